```python
import math
import jax, jax.numpy as jnp
from jax import lax
import numpy as np

D_MODEL = 1024
BATCH = 32
SEQ = 2048
DEPTH = 2

GRID_W = 64
CTX_LEN = 256
N_HEADS = 8
N_KV_HEADS = 2
HEAD_DIM = D_MODEL // N_HEADS
Q_PER_KV = N_HEADS // N_KV_HEADS
ROPE_THETA = 10000.0
Q_BLOCK = 128
CONV_WIDTH = D_MODEL // 2
CONV_TAPS = 3
SGU_WIDTH = D_MODEL // 2
SGU_GROUPS = 8
SGU_GROUP_DIM = SGU_WIDTH // SGU_GROUPS
CHUNK = 128
MIX_IN = 3 * CONV_WIDTH + 2 * SGU_WIDTH
MIX_OUT = CONV_WIDTH + SGU_WIDTH
FFN_HIDDEN = -(-8 * D_MODEL // (3 * 256)) * 256
ALPHA = (2.0 * DEPTH) ** 0.25
BETA = (8.0 * DEPTH) ** -0.25
N_EVEN = (DEPTH + 1) // 2
N_ODD = DEPTH // 2
LN_EPS = 1e-5
RMS_EPS = 1e-6

kernel_name = "hybrid_conv_sgu_gqa_prefix_deepnorm"


def _layernorm(x, g, b):
    xf = x.astype(jnp.float32)
    mu = jnp.mean(xf, axis=-1, keepdims=True)
    var = jnp.mean(jnp.square(xf - mu), axis=-1, keepdims=True)
    y = (xf - mu) * lax.rsqrt(var + LN_EPS) * g.astype(jnp.float32) + b.astype(jnp.float32)
    return y.astype(x.dtype)


def _rmsnorm(x, g):
    xf = x.astype(jnp.float32)
    y = xf * lax.rsqrt(jnp.mean(jnp.square(xf), axis=-1, keepdims=True) + RMS_EPS) * g.astype(jnp.float32)
    return y.astype(x.dtype)


def _modulate(h, shift, scale):
    return h * (1.0 + scale) + shift


def _post_norm(x, y, gate, g, b):
    return _layernorm(ALPHA * x + gate * y, g, b)


def _swiglu(h, w_in, w_out):
    gt, up = jnp.split(h @ w_in, 2, axis=-1)
    return (jax.nn.silu(gt) * up) @ w_out


def _sgu(u, v, ln_g, ln_b, w_s, b_s):
    bsz, n, _ = v.shape
    v = _layernorm(v, ln_g, ln_b).reshape(bsz, n // CHUNK, CHUNK, SGU_GROUPS, SGU_GROUP_DIM)
    s = jnp.einsum('gpq,bcqgd->bcpgd', w_s, v) + b_s.T[None, None, :, :, None]
    return u * s.reshape(bsz, n, SGU_WIDTH)


def _conv_chunk_mixer(h, w_in, conv_w, sgu_ln_g, sgu_ln_b, sgu_w, sgu_b, w_out):
    p = h @ w_in
    g_b, g_c, hv, u, v = jnp.split(
        p, [CONV_WIDTH, 2 * CONV_WIDTH, 3 * CONV_WIDTH, 3 * CONV_WIDTH + SGU_WIDTH], axis=-1)
    z = g_c * hv
    zp = jnp.pad(z, ((0, 0), (1, 1), (0, 0)))
    zc = conv_w[0] * zp[:, :-2] + conv_w[1] * zp[:, 1:-1] + conv_w[2] * zp[:, 2:]
    y_a = g_b * zc
    y_b = _sgu(jax.nn.gelu(u, approximate=False), jax.nn.gelu(v, approximate=False),
               sgu_ln_g, sgu_ln_b, sgu_w, sgu_b)
    return jnp.concatenate([y_a, y_b], axis=-1) @ w_out


def _axial_tables(n):
    rows = n // GRID_W
    row = jnp.repeat(jnp.arange(rows, dtype=jnp.float32), GRID_W)
    col = jnp.tile(jnp.arange(GRID_W, dtype=jnp.float32), rows)
    n_freq = HEAD_DIM // 4
    inv = ROPE_THETA ** (-jnp.arange(n_freq, dtype=jnp.float32) / n_freq)
    ang_r = row[:, None] * inv
    ang_c = col[:, None] * inv
    return (jnp.cos(ang_r)[None, :, None, :], jnp.sin(ang_r)[None, :, None, :],
            jnp.cos(ang_c)[None, :, None, :], jnp.sin(ang_c)[None, :, None, :])


def _rotate(x, cos, sin):
    x1, x2 = jnp.split(x, 2, axis=-1)
    return jnp.concatenate([x1 * cos - x2 * sin, x2 * cos + x1 * sin], axis=-1)


def _axial_rope(x, tables):
    cos_r, sin_r, cos_c, sin_c = tables
    xf = x.astype(jnp.float32)
    xr, xc = jnp.split(xf, 2, axis=-1)
    return jnp.concatenate([_rotate(xr, cos_r, sin_r), _rotate(xc, cos_c, sin_c)], axis=-1).astype(x.dtype)


def _proj_q(h, w_q, q_g):
    bsz, n, _ = h.shape
    return _rmsnorm((h @ w_q).reshape(bsz, n, N_HEADS, HEAD_DIM), q_g)


def _proj_kv(h, w_kv, k_g):
    bsz, n, _ = h.shape
    k, v = jnp.split((h @ w_kv).reshape(bsz, n, 2 * N_KV_HEADS, HEAD_DIM), 2, axis=2)
    return _rmsnorm(k, k_g), v


def _attend(q, k, v):
    bsz, n, _, _ = q.shape
    nb = n // Q_BLOCK
    qb = jnp.moveaxis(q.reshape(bsz, nb, Q_BLOCK, N_KV_HEADS, Q_PER_KV, HEAD_DIM), 1, 0)
    scale = HEAD_DIM ** -0.5

    def block(qi):
        s = jnp.einsum('bqkgd,bskd->bkgqs', qi, k).astype(jnp.float32) * scale
        p = jax.nn.softmax(s, axis=-1).astype(v.dtype)
        return jnp.einsum('bkgqs,bskd->bqkgd', p, v)

    o = lax.map(block, qb)
    return jnp.moveaxis(o, 0, 1).reshape(bsz, n, N_HEADS * HEAD_DIM)


def _attention_mixer(h_lat, h_ctx, w_qkv, q_g, k_g, w_out, with_ctx_out):
    w_q = w_qkv[:, :N_HEADS * HEAD_DIM]
    w_kv = w_qkv[:, N_HEADS * HEAD_DIM:]
    tables = _axial_tables(h_lat.shape[1])
    q_l = _axial_rope(_proj_q(h_lat, w_q, q_g), tables)
    k_l, v_l = _proj_kv(h_lat, w_kv, k_g)
    k_l = _axial_rope(k_l, tables)
    k_c, v_c = _proj_kv(h_ctx, w_kv, k_g)
    k_all = jnp.concatenate([k_l, k_c], axis=1)
    v_all = jnp.concatenate([v_l, v_c], axis=1)
    y_lat = _attend(q_l, k_all, v_all) @ w_out
    y_ctx = _attend(_proj_q(h_ctx, w_q, q_g), k_c, v_c) @ w_out if with_ctx_out else None
    return y_lat, y_ctx


def setup_inputs(seed: int = 0) -> dict:
    key = jax.random.key(seed)
    ks = jax.random.split(key, 21)
    f32 = jnp.float32
    nrm = lambda k, shape, s: jax.random.normal(k, shape, f32) * s
    D = D_MODEL
    return {
        "x": nrm(ks[0], (BATCH, SEQ, D), 1.0),
        "c": nrm(ks[1], (BATCH, D), 1.0),
        "ctx": nrm(ks[2], (BATCH, CTX_LEN, D), 1.0),
        "c_ctx": nrm(ks[3], (D,), 1.0),
        "ada_w": nrm(ks[4], (DEPTH, D, 6 * D), 0.5 * D ** -0.5),
        "ada_b": nrm(ks[5], (DEPTH, 6 * D), 0.01),
        "ln_g": 1.0 + nrm(ks[6], (DEPTH, 2, D), 0.05),
        "ln_b": nrm(ks[7], (DEPTH, 2, D), 0.01),
        "ffn_w_in": nrm(ks[8], (DEPTH, D, 2 * FFN_HIDDEN), D ** -0.5),
        "ffn_w_out": nrm(ks[9], (DEPTH, FFN_HIDDEN, D), BETA * FFN_HIDDEN ** -0.5),
        "mix_w_in": nrm(ks[10], (N_EVEN, D, MIX_IN), D ** -0.5),
        "conv_w": nrm(ks[11], (N_EVEN, CONV_TAPS, CONV_WIDTH), CONV_TAPS ** -0.5),
        "sgu_ln_g": 1.0 + nrm(ks[12], (N_EVEN, SGU_WIDTH), 0.05),
        "sgu_ln_b": nrm(ks[13], (N_EVEN, SGU_WIDTH), 0.01),
        "sgu_w": nrm(ks[14], (N_EVEN, SGU_GROUPS, CHUNK, CHUNK), CHUNK ** -0.5),
        "sgu_b": 1.0 + nrm(ks[15], (N_EVEN, SGU_GROUPS, CHUNK), 0.01),
        "mix_w_out": nrm(ks[16], (N_EVEN, MIX_OUT, D), BETA * MIX_OUT ** -0.5),
        "attn_w_qkv": nrm(ks[17], (N_ODD, D, (N_HEADS + 2 * N_KV_HEADS) * HEAD_DIM), D ** -0.5),
        "q_norm_g": 1.0 + nrm(ks[18], (N_ODD, HEAD_DIM), 0.05),
        "k_norm_g": 1.0 + nrm(ks[19], (N_ODD, HEAD_DIM), 0.05),
        "attn_w_out": nrm(ks[20], (N_ODD, N_HEADS * HEAD_DIM, D), BETA * (N_HEADS * HEAD_DIM) ** -0.5),
    }


def reference(x, c, ctx, c_ctx, ada_w, ada_b, ln_g, ln_b, ffn_w_in, ffn_w_out,
              mix_w_in, conv_w, sgu_ln_g, sgu_ln_b, sgu_w, sgu_b, mix_w_out,
              attn_w_qkv, q_norm_g, k_norm_g, attn_w_out):
    s_lat = jax.nn.silu(c)
    s_ctx = jax.nn.silu(c_ctx)
    for l in range(DEPTH):
        last = l == DEPTH - 1
        m_lat = jnp.split((s_lat @ ada_w[l] + ada_b[l])[:, None, :], 6, axis=-1)
        m_ctx = jnp.split(s_ctx @ ada_w[l] + ada_b[l], 6, axis=-1)
        h_lat = _modulate(x, m_lat[0], m_lat[1])
        h_ctx = _modulate(ctx, m_ctx[0], m_ctx[1])
        if l % 2 == 0:
            e = l // 2
            params = (mix_w_in[e], conv_w[e], sgu_ln_g[e], sgu_ln_b[e], sgu_w[e], sgu_b[e], mix_w_out[e])
            y_lat = _conv_chunk_mixer(h_lat, *params)
            y_ctx = None if last else _conv_chunk_mixer(h_ctx, *params)
        else:
            o = l // 2
            y_lat, y_ctx = _attention_mixer(h_lat, h_ctx, attn_w_qkv[o], q_norm_g[o], k_norm_g[o],
                                            attn_w_out[o], not last)
        x = _post_norm(x, y_lat, m_lat[2], ln_g[l, 0], ln_b[l, 0])
        x = _post_norm(x, _swiglu(_modulate(x, m_lat[3], m_lat[4]), ffn_w_in[l], ffn_w_out[l]),
                       m_lat[5], ln_g[l, 1], ln_b[l, 1])
        if not last:
            ctx = _post_norm(ctx, y_ctx, m_ctx[2], ln_g[l, 0], ln_b[l, 0])
            ctx = _post_norm(ctx, _swiglu(_modulate(ctx, m_ctx[3], m_ctx[4]), ffn_w_in[l], ffn_w_out[l]),
                             m_ctx[5], ln_g[l, 1], ln_b[l, 1])
    return x
```

```python
import functools
import math

import jax
import jax.numpy as jnp
from jax import lax
from jax.experimental import pallas as pl
from jax.experimental.pallas import tpu as pltpu

F32 = jnp.float32
BF16 = jnp.bfloat16

GRID_W = 64
N_HEADS = 8
N_KV_HEADS = 2
Q_PER_KV = N_HEADS // N_KV_HEADS
HEAD_DIM = 128
ROPE_THETA = 10000.0
CONV_TAPS = 3
SGU_GROUPS = 8
CHUNK = 128
LN_EPS = 1e-5
RMS_EPS = 1e-6
N_MOD = 6
MOD_ROWS_PAD = 8

V7X_VMEM_LIMIT_BYTES = 56 * 1024 * 1024
BF16_SUBLANES = 16

TILE_MIX = 512
TILE_FFN = 512
TILE_QKV = 512
TILE_ATTN = 256
FFN_HIDDEN_CHUNKS = 2


def _params(n_grid_dims):
    return pltpu.CompilerParams(
        dimension_semantics=("arbitrary",) * n_grid_dims,
        vmem_limit_bytes=V7X_VMEM_LIMIT_BYTES)


def _resident(shape, index_map):
    return pl.BlockSpec(shape, index_map, pipeline_mode=pl.Buffered(1))


def _layernorm(xf, g, b):
    mu = jnp.mean(xf, axis=-1, keepdims=True)
    xc = xf - mu
    var = jnp.mean(xc * xc, axis=-1, keepdims=True)
    return xc * lax.rsqrt(var + LN_EPS) * g + b


def _gelu(x):
    return 0.5 * x * (1.0 + lax.erf(x * (1.0 / math.sqrt(2.0))))


def _silu(x):
    return x * jax.nn.sigmoid(x)


def _modulated_bf16(x, mod_ref, shift_row):
    shift = mod_ref[shift_row:shift_row + 1, :]
    scale = mod_ref[shift_row + 1:shift_row + 2, :]
    return (x * (1.0 + scale) + shift).astype(BF16)


def _ada_kernel(c_ref, w_ref, b_ref, o_ref):
    s = _silu(c_ref[...])
    o_ref[...] = jnp.dot(s, w_ref[...], preferred_element_type=F32,
                         precision=lax.Precision.HIGHEST) + b_ref[...]


def _ada_modulation(c_all, ada_w, ada_b):
    depth, d, n6 = ada_w.shape
    rows = c_all.shape[0]
    tn = d
    out = pl.pallas_call(
        _ada_kernel,
        grid=(depth, n6 // tn),
        in_specs=[
            pl.BlockSpec((rows, d), lambda l, j: (0, 0)),
            pl.BlockSpec((None, d, tn), lambda l, j: (l, 0, j)),
            pl.BlockSpec((None, 1, tn), lambda l, j: (l, 0, j)),
        ],
        out_specs=pl.BlockSpec((None, rows, tn), lambda l, j: (l, 0, j)),
        out_shape=jax.ShapeDtypeStruct((depth, rows, n6), F32),
        compiler_params=_params(2),
        name="ada_modulation",
    )(c_all, ada_w, ada_b.reshape(depth, 1, n6))
    return out.reshape(depth, rows, N_MOD, d)


def _mix_in_kernel(x_ref, mod_ref, w_ref, lng_ref, lnb_ref, o_ref, *, cw):
    h = _modulated_bf16(x_ref[...], mod_ref, 0)
    p = jnp.dot(h, w_ref[...], preferred_element_type=F32)
    o_ref[:, 0:cw] = p[:, 0:cw].astype(BF16)
    o_ref[:, cw:2 * cw] = (p[:, cw:2 * cw] * p[:, 2 * cw:3 * cw]).astype(BF16)
    o_ref[:, 2 * cw:3 * cw] = _gelu(p[:, 3 * cw:4 * cw]).astype(BF16)
    v = _layernorm(_gelu(p[:, 4 * cw:5 * cw]), lng_ref[...], lnb_ref[...])
    o_ref[:, 3 * cw:4 * cw] = v.astype(BF16)


def _mix_in(x, mod, mod_row, layer, w_in, sgu_ln_g, sgu_ln_b, e, tile):
    bsz, n, d = x.shape
    cw = w_in.shape[-1] // 5
    mod_map = (lambda b, i: (layer, b, 0, 0)) if mod_row is None else (lambda b, i: (layer, mod_row, 0, 0))
    return pl.pallas_call(
        functools.partial(_mix_in_kernel, cw=cw),
        grid=(bsz, n // tile),
        in_specs=[
            pl.BlockSpec((None, tile, d), lambda b, i: (b, i, 0)),
            pl.BlockSpec((None, None, N_MOD, d), mod_map),
            _resident((None, d, 5 * cw), lambda b, i: (e, 0, 0)),
            _resident((None, 1, cw), lambda b, i: (e, 0, 0)),
            _resident((None, 1, cw), lambda b, i: (e, 0, 0)),
        ],
        out_specs=pl.BlockSpec((None, tile, 4 * cw), lambda b, i: (b, i, 0)),
        out_shape=jax.ShapeDtypeStruct((bsz, n, 4 * cw), BF16),
        compiler_params=_params(2),
        name="mix_in",
    )(x, mod, w_in, sgu_ln_g, sgu_ln_b)


def _mix_out_kernel(q_ref, zprev_ref, znext_ref, x_ref, mod_ref, convw_ref, wcat_ref, sb_ref,
                    wout_ref, lng_ref, lnb_ref, o_ref, y_scr, *, cw, tile, alpha):
    i = pl.program_id(1)
    n_tiles = pl.num_programs(1)

    z = q_ref[:, cw:2 * cw].astype(F32)
    rows = lax.broadcasted_iota(jnp.int32, (tile, cw), 0)
    prev_row = jnp.where(i > 0, zprev_ref[BF16_SUBLANES - 1:BF16_SUBLANES, :].astype(F32), 0.0)
    next_row = jnp.where(i < n_tiles - 1, znext_ref[0:1, :].astype(F32), 0.0)
    z_before = jnp.where(rows == 0, prev_row, pltpu.roll(z, 1, axis=0))
    z_after = jnp.where(rows == tile - 1, next_row, pltpu.roll(z, tile - 1, axis=0))
    zc = convw_ref[0:1, :] * z_before + convw_ref[1:2, :] * z + convw_ref[2:3, :] * z_after
    y_scr[:, 0:cw] = (q_ref[:, 0:cw].astype(F32) * zc).astype(BF16)

    n_groups = wcat_ref.shape[1] // CHUNK
    gdim = cw // n_groups
    r_grp = lax.broadcasted_iota(jnp.int32, (n_groups * CHUNK, cw), 0) // CHUNK
    c_grp = lax.broadcasted_iota(jnp.int32, (n_groups * CHUNK, cw), 1) // gdim
    diag = r_grp == c_grp
    for c in range(tile // CHUNK):
        sl = slice(c * CHUNK, (c + 1) * CHUNK)
        v = q_ref[sl, 3 * cw:4 * cw]
        vd = jnp.where(diag, jnp.concatenate([v] * n_groups, axis=0), jnp.zeros((), BF16))
        s = jnp.dot(wcat_ref[...], vd, preferred_element_type=F32) + sb_ref[...]
        y_scr[sl, cw:2 * cw] = (q_ref[sl, 2 * cw:3 * cw].astype(F32) * s).astype(BF16)

    y = jnp.dot(y_scr[...], wout_ref[...], preferred_element_type=F32)
    gate = mod_ref[2:3, :]
    o_ref[...] = _layernorm(alpha * x_ref[...] + gate * y, lng_ref[0:1, :], lnb_ref[0:1, :])


def _mix_out(q, x, mod, mod_row, layer, conv_w, wcat, sbias, w_out, ln_g, ln_b, e, tile, alpha):
    bsz, n, d = x.shape
    cw = q.shape[-1] // 4
    halo = BF16_SUBLANES
    per_tile = tile // halo
    n_halo_blocks = n // halo
    mod_map = (lambda b, i: (layer, b, 0, 0)) if mod_row is None else (lambda b, i: (layer, mod_row, 0, 0))
    return pl.pallas_call(
        functools.partial(_mix_out_kernel, cw=cw, tile=tile, alpha=alpha),
        grid=(bsz, n // tile),
        in_specs=[
            pl.BlockSpec((None, tile, 4 * cw), lambda b, i: (b, i, 0)),
            pl.BlockSpec((None, halo, cw), lambda b, i: (b, jnp.maximum(i * per_tile - 1, 0), 1)),
            pl.BlockSpec((None, halo, cw),
                         lambda b, i: (b, jnp.minimum((i + 1) * per_tile, n_halo_blocks - 1), 1)),
            pl.BlockSpec((None, tile, d), lambda b, i: (b, i, 0)),
            pl.BlockSpec((None, None, N_MOD, d), mod_map),
            _resident((None, CONV_TAPS, cw), lambda b, i: (e, 0, 0)),
            _resident((None, CHUNK, wcat.shape[-1]), lambda b, i: (e, 0, 0)),
            _resident((None, CHUNK, cw), lambda b, i: (e, 0, 0)),
            _resident((None, 2 * cw, d), lambda b, i: (e, 0, 0)),
            _resident((None, 2, d), lambda b, i: (layer, 0, 0)),
            _resident((None, 2, d), lambda b, i: (layer, 0, 0)),
        ],
        out_specs=pl.BlockSpec((None, tile, d), lambda b, i: (b, i, 0)),
        out_shape=jax.ShapeDtypeStruct((bsz, n, d), F32),
        scratch_shapes=[pltpu.VMEM((tile, 2 * cw), BF16)],
        compiler_params=_params(2),
        name="mix_out",
    )(q, q, q, x, mod, conv_w, wcat, sbias, w_out, ln_g, ln_b)


def _ffn_kernel(x_ref, mod_ref, win_ref, wout_ref, lng_ref, lnb_ref, o_ref, *, hidden, n_chunks, alpha):
    x = x_ref[...]
    h = _modulated_bf16(x, mod_ref, 3)
    hc = hidden // n_chunks
    y = None
    for c in range(n_chunks):
        gt = jnp.dot(h, win_ref[:, c * hc:(c + 1) * hc], preferred_element_type=F32)
        up = jnp.dot(h, win_ref[:, hidden + c * hc:hidden + (c + 1) * hc], preferred_element_type=F32)
        a = (_silu(gt) * up).astype(BF16)
        part = jnp.dot(a, wout_ref[c * hc:(c + 1) * hc, :], preferred_element_type=F32)
        y = part if y is None else y + part
    gate = mod_ref[5:6, :]
    o_ref[...] = _layernorm(alpha * x + gate * y, lng_ref[1:2, :], lnb_ref[1:2, :])


def _ffn(x, mod, mod_row, layer, w_in, w_out, ln_g, ln_b, tile, alpha):
    bsz, n, d = x.shape
    hidden = w_out.shape[1]
    mod_map = (lambda b, i: (layer, b, 0, 0)) if mod_row is None else (lambda b, i: (layer, mod_row, 0, 0))
    return pl.pallas_call(
        functools.partial(_ffn_kernel, hidden=hidden, n_chunks=FFN_HIDDEN_CHUNKS, alpha=alpha),
        grid=(bsz, n // tile),
        in_specs=[
            pl.BlockSpec((None, tile, d), lambda b, i: (b, i, 0)),
            pl.BlockSpec((None, None, N_MOD, d), mod_map),
            _resident((None, d, 2 * hidden), lambda b, i: (layer, 0, 0)),
            _resident((None, hidden, d), lambda b, i: (layer, 0, 0)),
            _resident((None, 2, d), lambda b, i: (layer, 0, 0)),
            _resident((None, 2, d), lambda b, i: (layer, 0, 0)),
        ],
        out_specs=pl.BlockSpec((None, tile, d), lambda b, i: (b, i, 0)),
        out_shape=jax.ShapeDtypeStruct((bsz, n, d), F32),
        compiler_params=_params(2),
        name="ffn",
    )(x, mod, w_in, w_out, ln_g, ln_b)


def _rms_head(xh, g):
    return xh * lax.rsqrt(jnp.mean(xh * xh, axis=-1, keepdims=True) + RMS_EPS) * g


def _rope_head(xh, cos, sin_signed, swap_lo):
    partner = jnp.where(swap_lo, pltpu.roll(xh, HEAD_DIM - HEAD_DIM // 4, axis=1),
                        pltpu.roll(xh, HEAD_DIM // 4, axis=1))
    return xh * cos + partner * sin_signed


def _qkv_kernel(x_ref, mod_ref, w_ref, qg_ref, kg_ref, cos_ref, sin_ref, *out_refs,
                with_q, with_rope, tile):
    h = _modulated_bf16(x_ref[...], mod_ref, 0)
    n_q = N_HEADS * HEAD_DIM
    n_kv = N_KV_HEADS * HEAD_DIM
    if with_rope:
        cos = cos_ref[...]
        sin_signed = sin_ref[...]
        lane = lax.broadcasted_iota(jnp.int32, (tile, HEAD_DIM), 1)
        swap_lo = (lane % (HEAD_DIM // 2)) < (HEAD_DIM // 4)
    if with_q:
        q_ref, k_ref, v_ref = out_refs
        pq = jnp.dot(h, w_ref[:, 0:n_q], preferred_element_type=F32)
        scale = HEAD_DIM ** -0.5
        for hh in range(N_HEADS):
            sl = slice(hh * HEAD_DIM, (hh + 1) * HEAD_DIM)
            qh = _rms_head(pq[:, sl], qg_ref[...])
            if with_rope:
                qh = _rope_head(qh, cos, sin_signed, swap_lo)
            q_ref[:, sl] = (qh * scale).astype(BF16)
    else:
        k_ref, v_ref = out_refs
    pkv = jnp.dot(h, w_ref[:, n_q:n_q + 2 * n_kv], preferred_element_type=F32)
    for hh in range(N_KV_HEADS):
        sl = slice(hh * HEAD_DIM, (hh + 1) * HEAD_DIM)
        kh = _rms_head(pkv[:, sl], kg_ref[...])
        if with_rope:
            kh = _rope_head(kh, cos, sin_signed, swap_lo)
        k_ref[:, sl] = kh.astype(BF16)
    v_ref[...] = pkv[:, n_kv:2 * n_kv].astype(BF16)


def _qkv(x, mod, mod_row, layer, w_qkv, q_g, k_g, cos, sin_signed, o, tile, with_q, with_rope):
    bsz, n, d = x.shape
    n_q = N_HEADS * HEAD_DIM
    n_kv = N_KV_HEADS * HEAD_DIM
    mod_map = (lambda b, i: (layer, b, 0, 0)) if mod_row is None else (lambda b, i: (layer, mod_row, 0, 0))
    kv_shape = jax.ShapeDtypeStruct((bsz, n, n_kv), BF16)
    kv_spec = pl.BlockSpec((None, tile, n_kv), lambda b, i: (b, i, 0))
    out_shape = [kv_shape, kv_shape]
    out_specs = [kv_spec, kv_spec]
    if with_q:
        out_shape = [jax.ShapeDtypeStruct((bsz, n, n_q), BF16)] + out_shape
        out_specs = [pl.BlockSpec((None, tile, n_q), lambda b, i: (b, i, 0))] + out_specs
    return pl.pallas_call(
        functools.partial(_qkv_kernel, with_q=with_q, with_rope=with_rope, tile=tile),
        grid=(bsz, n // tile),
        in_specs=[
            pl.BlockSpec((None, tile, d), lambda b, i: (b, i, 0)),
            pl.BlockSpec((None, None, N_MOD, d), mod_map),
            _resident((None, d, n_q + 2 * n_kv), lambda b, i: (o, 0, 0)),
            _resident((None, 1, HEAD_DIM), lambda b, i: (o, 0, 0)),
            _resident((None, 1, HEAD_DIM), lambda b, i: (o, 0, 0)),
            pl.BlockSpec((tile, HEAD_DIM), lambda b, i: (i, 0)),
            pl.BlockSpec((tile, HEAD_DIM), lambda b, i: (i, 0)),
        ],
        out_specs=out_specs,
        out_shape=out_shape,
        compiler_params=_params(2),
        name="qkv" if with_q else "kv_ctx",
    )(x, mod, w_qkv, q_g, k_g, cos, sin_signed)


def _rope_tables(n):
    rows = n // GRID_W
    row = jnp.repeat(jnp.arange(rows, dtype=F32), GRID_W)
    col = jnp.tile(jnp.arange(GRID_W, dtype=F32), rows)
    n_freq = HEAD_DIM // 4
    inv = ROPE_THETA ** (-jnp.arange(n_freq, dtype=F32) / n_freq)
    ang_r = row[:, None] * inv
    ang_c = col[:, None] * inv
    cos = jnp.concatenate([jnp.cos(ang_r), jnp.cos(ang_r), jnp.cos(ang_c), jnp.cos(ang_c)], axis=-1)
    sin = jnp.concatenate([-jnp.sin(ang_r), jnp.sin(ang_r), -jnp.sin(ang_c), jnp.sin(ang_c)], axis=-1)
    return cos, sin


def _attn_kernel(q_ref, kl_ref, vl_ref, kc_ref, vc_ref, x_ref, mod_ref, wout_ref, lng_ref, lnb_ref,
                 o_ref, o_scr, *, alpha):
    contract_last = (((1,), (1,)), ((), ()))
    for kvh in range(N_KV_HEADS):
        ksl = slice(kvh * HEAD_DIM, (kvh + 1) * HEAD_DIM)
        k_lat = kl_ref[:, ksl]
        k_ctx = kc_ref[:, ksl]
        v_lat = vl_ref[:, ksl]
        v_ctx = vc_ref[:, ksl]
        for g in range(Q_PER_KV):
            hh = kvh * Q_PER_KV + g
            sl = slice(hh * HEAD_DIM, (hh + 1) * HEAD_DIM)
            qh = q_ref[:, sl]
            s_lat = lax.dot_general(qh, k_lat, contract_last, preferred_element_type=F32)
            s_ctx = lax.dot_general(qh, k_ctx, contract_last, preferred_element_type=F32)
            m = jnp.maximum(jnp.max(s_lat, axis=-1, keepdims=True),
                            jnp.max(s_ctx, axis=-1, keepdims=True))
            p_lat = jnp.exp(s_lat - m)
            p_ctx = jnp.exp(s_ctx - m)
            denom = jnp.sum(p_lat, axis=-1, keepdims=True) + jnp.sum(p_ctx, axis=-1, keepdims=True)
            o = (jnp.dot(p_lat.astype(BF16), v_lat, preferred_element_type=F32)
                 + jnp.dot(p_ctx.astype(BF16), v_ctx, preferred_element_type=F32))
            o_scr[:, sl] = (o / denom).astype(BF16)
    y = jnp.dot(o_scr[...], wout_ref[...], preferred_element_type=F32)
    gate = mod_ref[2:3, :]
    o_ref[...] = _layernorm(alpha * x_ref[...] + gate * y, lng_ref[0:1, :], lnb_ref[0:1, :])


def _attention(q, k_lat, v_lat, k_ctx, v_ctx, x, mod, layer, w_out, ln_g, ln_b, o, tile, alpha):
    bsz, n, d = x.shape
    n_ctx = k_ctx.shape[1]
    n_q = q.shape[-1]
    n_kv = k_lat.shape[-1]
    return pl.pallas_call(
        functools.partial(_attn_kernel, alpha=alpha),
        grid=(bsz, n // tile),
        in_specs=[
            pl.BlockSpec((None, tile, n_q), lambda b, i: (b, i, 0)),
            pl.BlockSpec((None, n, n_kv), lambda b, i: (b, 0, 0)),
            pl.BlockSpec((None, n, n_kv), lambda b, i: (b, 0, 0)),
            pl.BlockSpec((None, n_ctx, n_kv), lambda b, i: (b, 0, 0)),
            pl.BlockSpec((None, n_ctx, n_kv), lambda b, i: (b, 0, 0)),
            pl.BlockSpec((None, tile, d), lambda b, i: (b, i, 0)),
            pl.BlockSpec((None, None, N_MOD, d), lambda b, i: (layer, b, 0, 0)),
            _resident((None, n_q, d), lambda b, i: (o, 0, 0)),
            _resident((None, 2, d), lambda b, i: (layer, 0, 0)),
            _resident((None, 2, d), lambda b, i: (layer, 0, 0)),
        ],
        out_specs=pl.BlockSpec((None, tile, d), lambda b, i: (b, i, 0)),
        out_shape=jax.ShapeDtypeStruct((bsz, n, d), F32),
        scratch_shapes=[pltpu.VMEM((tile, n_q), BF16)],
        compiler_params=_params(2),
        name="attention",
    )(q, k_lat, v_lat, k_ctx, v_ctx, x, mod, w_out, ln_g, ln_b)


def kernel(x, c, ctx, c_ctx, ada_w, ada_b, ln_g, ln_b, ffn_w_in, ffn_w_out, mix_w_in, conv_w,
           sgu_ln_g, sgu_ln_b, sgu_w, sgu_b, mix_w_out, attn_w_qkv, q_norm_g, k_norm_g, attn_w_out):
    bsz, seq, d = x.shape
    n_ctx = ctx.shape[1]
    depth = ada_w.shape[0]
    alpha = (2.0 * depth) ** 0.25
    assert seq % TILE_MIX == 0 and seq % TILE_QKV == 0 and seq % TILE_ATTN == 0 and seq % TILE_FFN == 0
    assert n_ctx % CHUNK == 0 and seq % GRID_W == 0
    tile_ctx = n_ctx

    pad = (-(bsz + 1)) % MOD_ROWS_PAD
    c_all = jnp.concatenate([c, c_ctx[None, :], jnp.zeros((pad, d), F32)], axis=0)
    ctx_row = bsz
    mod = _ada_modulation(c_all, ada_w, ada_b)

    ffn_w_in_b = ffn_w_in.astype(BF16)
    ffn_w_out_b = ffn_w_out.astype(BF16)
    mix_w_in_b = mix_w_in.astype(BF16)
    mix_w_out_b = mix_w_out.astype(BF16)
    attn_w_qkv_b = attn_w_qkv.astype(BF16)
    attn_w_out_b = attn_w_out.astype(BF16)
    n_even, n_groups = sgu_w.shape[0], sgu_w.shape[1]
    cw = sgu_ln_g.shape[-1]
    sgu_wcat = jnp.transpose(sgu_w, (0, 2, 1, 3)).reshape(n_even, CHUNK, n_groups * CHUNK).astype(BF16)
    sgu_bias = jnp.repeat(jnp.transpose(sgu_b, (0, 2, 1)), cw // n_groups, axis=-1)
    sgu_ln_g3 = sgu_ln_g.reshape(n_even, 1, cw)
    sgu_ln_b3 = sgu_ln_b.reshape(n_even, 1, cw)
    q_g3 = q_norm_g.reshape(-1, 1, HEAD_DIM)
    k_g3 = k_norm_g.reshape(-1, 1, HEAD_DIM)
    cos, sin_signed = _rope_tables(seq)

    for l in range(depth):
        last = l == depth - 1
        if l % 2 == 0:
            e = l // 2
            mix_args = (conv_w, sgu_wcat, sgu_bias, mix_w_out_b, ln_g, ln_b, e)
            q_lat = _mix_in(x, mod, None, l, mix_w_in_b, sgu_ln_g3, sgu_ln_b3, e, TILE_MIX)
            x_mid = _mix_out(q_lat, x, mod, None, l, *mix_args, TILE_MIX, alpha)
            if not last:
                q_ctx = _mix_in(ctx, mod, ctx_row, l, mix_w_in_b, sgu_ln_g3, sgu_ln_b3, e, tile_ctx)
                ctx_mid = _mix_out(q_ctx, ctx, mod, ctx_row, l, *mix_args, tile_ctx, alpha)
        else:
            o = l // 2
            assert last, "an attention layer that also updates the context stream is not implemented"
            q, k_lat, v_lat = _qkv(x, mod, None, l, attn_w_qkv_b, q_g3, k_g3, cos, sin_signed, o,
                                   TILE_QKV, True, True)
            k_ctx, v_ctx = _qkv(ctx, mod, ctx_row, l, attn_w_qkv_b, q_g3, k_g3, cos, sin_signed, o,
                                tile_ctx, False, False)
            x_mid = _attention(q, k_lat, v_lat, k_ctx, v_ctx, x, mod, l, attn_w_out_b, ln_g, ln_b, o,
                               TILE_ATTN, alpha)
        x = _ffn(x_mid, mod, None, l, ffn_w_in_b, ffn_w_out_b, ln_g, ln_b, TILE_FFN, alpha)
        if not last:
            ctx = _ffn(ctx_mid, mod, ctx_row, l, ffn_w_in_b, ffn_w_out_b, ln_g, ln_b, tile_ctx, alpha)
    return x
```

```python
import functools
import math

import jax
import jax.numpy as jnp
from jax import lax
from jax.experimental import pallas as pl
from jax.experimental.pallas import tpu as pltpu

F32 = jnp.float32
BF16 = jnp.bfloat16

GRID_W = 64
N_HEADS = 8
N_KV_HEADS = 2
Q_PER_KV = N_HEADS // N_KV_HEADS
HEAD_DIM = 128
ROPE_THETA = 10000.0
CONV_TAPS = 3
SGU_GROUPS = 8
CHUNK = 128
LN_EPS = 1e-5
RMS_EPS = 1e-6
N_MOD = 6
MOD_ROWS_PAD = 8

V7X_VMEM_LIMIT_BYTES = 56 * 1024 * 1024
V7X_MXU_DIM = 256
BF16_SUBLANES = 16

TILE_MIX = 512
TILE_FFN = 1024
TILE_KV = 512
TILE_ATTN = 512
FFN_SUB_ROWS = 256
MIX_SUB_ROWS = 256
FFN_HIDDEN_CHUNKS = 2


def _params(n_grid_dims):
    return pltpu.CompilerParams(
        dimension_semantics=("arbitrary",) * n_grid_dims,
        vmem_limit_bytes=V7X_VMEM_LIMIT_BYTES)


def _resident(shape, index_map):
    return pl.BlockSpec(shape, index_map, pipeline_mode=pl.Buffered(1))


def _layernorm(xf, g, b):
    mu = jnp.mean(xf, axis=-1, keepdims=True)
    xc = xf - mu
    var = jnp.mean(xc * xc, axis=-1, keepdims=True)
    return xc * lax.rsqrt(var + LN_EPS) * g + b


def _gelu(x):
    return 0.5 * x * (1.0 + lax.erf(x * (1.0 / math.sqrt(2.0))))


def _silu(x):
    return x * jax.nn.sigmoid(x)


def _modulated_bf16(x, mod_ref, shift_row):
    shift = mod_ref[shift_row:shift_row + 1, :]
    scale = mod_ref[shift_row + 1:shift_row + 2, :]
    return (x * (1.0 + scale) + shift).astype(BF16)


def _ada_kernel(c_ref, w_ref, b_ref, o_ref):
    s = _silu(c_ref[...])
    o_ref[...] = jnp.dot(s, w_ref[...], preferred_element_type=F32,
                         precision=lax.Precision.HIGHEST) + b_ref[...]


def _ada_modulation(c_all, ada_w, ada_b):
    depth, d, n6 = ada_w.shape
    rows = c_all.shape[0]
    tn = d
    out = pl.pallas_call(
        _ada_kernel,
        grid=(depth, n6 // tn),
        in_specs=[
            pl.BlockSpec((rows, d), lambda l, j: (0, 0)),
            pl.BlockSpec((None, d, tn), lambda l, j: (l, 0, j)),
            pl.BlockSpec((None, 1, tn), lambda l, j: (l, 0, j)),
        ],
        out_specs=pl.BlockSpec((None, rows, tn), lambda l, j: (l, 0, j)),
        out_shape=jax.ShapeDtypeStruct((depth, rows, n6), F32),
        compiler_params=_params(2),
        name="ada_modulation",
    )(c_all, ada_w, ada_b.reshape(depth, 1, n6))
    return out.reshape(depth, rows, N_MOD, d)


def _mix_in_kernel(x_ref, mod_ref, w_ref, lng_ref, lnb_ref, o_ref, *, cw, sub):
    for r in range(x_ref.shape[0] // sub):
        rows = slice(r * sub, (r + 1) * sub)
        h = _modulated_bf16(x_ref[rows, :], mod_ref, 0)
        p = jnp.dot(h, w_ref[...], preferred_element_type=F32)
        o_ref[rows, 0:cw] = p[:, 0:cw].astype(BF16)
        o_ref[rows, cw:2 * cw] = (p[:, cw:2 * cw] * p[:, 2 * cw:3 * cw]).astype(BF16)
        o_ref[rows, 2 * cw:3 * cw] = _gelu(p[:, 3 * cw:4 * cw]).astype(BF16)
        v = _layernorm(_gelu(p[:, 4 * cw:5 * cw]), lng_ref[...], lnb_ref[...])
        o_ref[rows, 3 * cw:4 * cw] = v.astype(BF16)


def _mix_in(x, mod, mod_row, layer, w_in, sgu_ln_g, sgu_ln_b, e, tile):
    bsz, n, d = x.shape
    cw = w_in.shape[-1] // 5
    mod_map = (lambda b, i: (layer, b, 0, 0)) if mod_row is None else (lambda b, i: (layer, mod_row, 0, 0))
    return pl.pallas_call(
        functools.partial(_mix_in_kernel, cw=cw, sub=min(MIX_SUB_ROWS, tile)),
        grid=(bsz, n // tile),
        in_specs=[
            pl.BlockSpec((None, tile, d), lambda b, i: (b, i, 0)),
            pl.BlockSpec((None, None, N_MOD, d), mod_map),
            _resident((None, d, 5 * cw), lambda b, i: (e, 0, 0)),
            _resident((None, 1, cw), lambda b, i: (e, 0, 0)),
            _resident((None, 1, cw), lambda b, i: (e, 0, 0)),
        ],
        out_specs=pl.BlockSpec((None, tile, 4 * cw), lambda b, i: (b, i, 0)),
        out_shape=jax.ShapeDtypeStruct((bsz, n, 4 * cw), BF16),
        compiler_params=_params(2),
        name="mix_in",
    )(x, mod, w_in, sgu_ln_g, sgu_ln_b)


def _mix_out_kernel(q_ref, zprev_ref, znext_ref, x_ref, mod_ref, convw_ref, wcat_ref, diag_ref, sb_ref,
                    wout_ref, lng_ref, lnb_ref, o_ref, y_scr, *, cw, tile, alpha):
    i = pl.program_id(1)
    n_tiles = pl.num_programs(1)

    z = q_ref[:, cw:2 * cw].astype(F32)
    rows = lax.broadcasted_iota(jnp.int32, (tile, cw), 0)
    prev_row = jnp.where(i > 0, zprev_ref[BF16_SUBLANES - 1:BF16_SUBLANES, :].astype(F32), 0.0)
    next_row = jnp.where(i < n_tiles - 1, znext_ref[0:1, :].astype(F32), 0.0)
    z_before = jnp.where(rows == 0, prev_row, pltpu.roll(z, 1, axis=0))
    z_after = jnp.where(rows == tile - 1, next_row, pltpu.roll(z, tile - 1, axis=0))
    zc = convw_ref[0:1, :] * z_before + convw_ref[1:2, :] * z + convw_ref[2:3, :] * z_after
    y_scr[:, 0:cw] = (q_ref[:, 0:cw].astype(F32) * zc).astype(BF16)

    gpt = diag_ref.shape[0] // CHUNK
    for c in range(tile // CHUNK):
        sl = slice(c * CHUNK, (c + 1) * CHUNK)
        for j in range(cw // V7X_MXU_DIM):
            lanes = slice(3 * cw + j * V7X_MXU_DIM, 3 * cw + (j + 1) * V7X_MXU_DIM)
            v = q_ref[sl, lanes]
            vd = jnp.concatenate([v] * gpt, axis=0) * diag_ref[...]
            w = wcat_ref[:, j * gpt * CHUNK:(j + 1) * gpt * CHUNK]
            out = slice(j * V7X_MXU_DIM, (j + 1) * V7X_MXU_DIM)
            s = jnp.dot(w, vd, preferred_element_type=F32) + sb_ref[:, out]
            u = q_ref[sl, 2 * cw + j * V7X_MXU_DIM:2 * cw + (j + 1) * V7X_MXU_DIM]
            y_scr[sl, cw + j * V7X_MXU_DIM:cw + (j + 1) * V7X_MXU_DIM] = (u.astype(F32) * s).astype(BF16)

    y = jnp.dot(y_scr[...], wout_ref[...], preferred_element_type=F32)
    gate = mod_ref[2:3, :]
    o_ref[...] = _layernorm(alpha * x_ref[...] + gate * y, lng_ref[0:1, :], lnb_ref[0:1, :])


def _mix_out(q, x, mod, mod_row, layer, conv_w, wcat, sbias, w_out, ln_g, ln_b, e, tile, alpha):
    bsz, n, d = x.shape
    cw = q.shape[-1] // 4
    n_groups = wcat.shape[-1] // CHUNK
    gpt = V7X_MXU_DIM // (cw // n_groups)
    diag = jnp.kron(jnp.eye(gpt, dtype=BF16), jnp.ones((CHUNK, V7X_MXU_DIM // gpt), BF16))
    halo = BF16_SUBLANES
    per_tile = tile // halo
    n_halo_blocks = n // halo
    mod_map = (lambda b, i: (layer, b, 0, 0)) if mod_row is None else (lambda b, i: (layer, mod_row, 0, 0))
    return pl.pallas_call(
        functools.partial(_mix_out_kernel, cw=cw, tile=tile, alpha=alpha),
        grid=(bsz, n // tile),
        in_specs=[
            pl.BlockSpec((None, tile, 4 * cw), lambda b, i: (b, i, 0)),
            pl.BlockSpec((None, halo, cw), lambda b, i: (b, jnp.maximum(i * per_tile - 1, 0), 1)),
            pl.BlockSpec((None, halo, cw),
                         lambda b, i: (b, jnp.minimum((i + 1) * per_tile, n_halo_blocks - 1), 1)),
            pl.BlockSpec((None, tile, d), lambda b, i: (b, i, 0)),
            pl.BlockSpec((None, None, N_MOD, d), mod_map),
            _resident((None, CONV_TAPS, cw), lambda b, i: (e, 0, 0)),
            _resident((None, CHUNK, wcat.shape[-1]), lambda b, i: (e, 0, 0)),
            _resident(diag.shape, lambda b, i: (0, 0)),
            _resident((None, CHUNK, cw), lambda b, i: (e, 0, 0)),
            _resident((None, 2 * cw, d), lambda b, i: (e, 0, 0)),
            _resident((None, 2, d), lambda b, i: (layer, 0, 0)),
            _resident((None, 2, d), lambda b, i: (layer, 0, 0)),
        ],
        out_specs=pl.BlockSpec((None, tile, d), lambda b, i: (b, i, 0)),
        out_shape=jax.ShapeDtypeStruct((bsz, n, d), F32),
        scratch_shapes=[pltpu.VMEM((tile, 2 * cw), BF16)],
        compiler_params=_params(2),
        name="mix_out",
    )(q, q, q, x, mod, conv_w, wcat, diag, sbias, w_out, ln_g, ln_b)


def _hidden_chunks(hidden, n_chunks):
    tiles = hidden // V7X_MXU_DIM
    assert tiles * V7X_MXU_DIM == hidden
    bounds = [V7X_MXU_DIM * ((tiles * c + n_chunks - 1) // n_chunks) for c in range(n_chunks + 1)]
    return tuple(zip(bounds[:-1], bounds[1:]))


def _ffn_kernel(x_ref, mod_ref, win_ref, wout_ref, lng_ref, lnb_ref, o_ref, *, hidden, chunks, sub, alpha):
    gate = mod_ref[5:6, :]
    for r in range(x_ref.shape[0] // sub):
        rows = slice(r * sub, (r + 1) * sub)
        x = x_ref[rows, :]
        h = _modulated_bf16(x, mod_ref, 3)
        y = None
        for c0, c1 in chunks:
            gt = jnp.dot(h, win_ref[:, c0:c1], preferred_element_type=F32)
            up = jnp.dot(h, win_ref[:, hidden + c0:hidden + c1], preferred_element_type=F32)
            a = (_silu(gt) * up).astype(BF16)
            part = jnp.dot(a, wout_ref[c0:c1, :], preferred_element_type=F32)
            y = part if y is None else y + part
        o_ref[rows, :] = _layernorm(alpha * x + gate * y, lng_ref[1:2, :], lnb_ref[1:2, :])


def _ffn(x, mod, mod_row, layer, w_in, w_out, ln_g, ln_b, tile, alpha):
    bsz, n, d = x.shape
    hidden = w_out.shape[1]
    mod_map = (lambda b, i: (layer, b, 0, 0)) if mod_row is None else (lambda b, i: (layer, mod_row, 0, 0))
    return pl.pallas_call(
        functools.partial(_ffn_kernel, hidden=hidden, chunks=_hidden_chunks(hidden, FFN_HIDDEN_CHUNKS),
                          sub=min(FFN_SUB_ROWS, tile), alpha=alpha),
        grid=(bsz, n // tile),
        in_specs=[
            pl.BlockSpec((None, tile, d), lambda b, i: (b, i, 0)),
            pl.BlockSpec((None, None, N_MOD, d), mod_map),
            _resident((None, d, 2 * hidden), lambda b, i: (layer, 0, 0)),
            _resident((None, hidden, d), lambda b, i: (layer, 0, 0)),
            _resident((None, 2, d), lambda b, i: (layer, 0, 0)),
            _resident((None, 2, d), lambda b, i: (layer, 0, 0)),
        ],
        out_specs=pl.BlockSpec((None, tile, d), lambda b, i: (b, i, 0)),
        out_shape=jax.ShapeDtypeStruct((bsz, n, d), F32),
        compiler_params=_params(2),
        name="ffn",
    )(x, mod, w_in, w_out, ln_g, ln_b)


def _rms_head(xh, g):
    return xh * lax.rsqrt(jnp.mean(xh * xh, axis=-1, keepdims=True) + RMS_EPS) * g


def _rope_head(xh, cos, sin_signed, swap_lo):
    partner = jnp.where(swap_lo, pltpu.roll(xh, HEAD_DIM - HEAD_DIM // 4, axis=1),
                        pltpu.roll(xh, HEAD_DIM // 4, axis=1))
    return xh * cos + partner * sin_signed


def _swap_lo_mask(rows):
    lane = lax.broadcasted_iota(jnp.int32, (rows, HEAD_DIM), 1)
    return (lane % (HEAD_DIM // 2)) < (HEAD_DIM // 4)


def _kv_kernel(x_ref, mod_ref, w_ref, kg_ref, cos_ref, sin_ref, kt_ref, v_ref, *, with_rope, tile):
    h = _modulated_bf16(x_ref[...], mod_ref, 0)
    n_kv = N_KV_HEADS * HEAD_DIM
    pkv = jnp.dot(h, w_ref[...], preferred_element_type=F32)
    for hh in range(N_KV_HEADS):
        sl = slice(hh * HEAD_DIM, (hh + 1) * HEAD_DIM)
        kh = _rms_head(pkv[:, sl], kg_ref[...])
        if with_rope:
            kh = _rope_head(kh, cos_ref[...], sin_ref[...], _swap_lo_mask(tile))
        kt_ref[sl, :] = kh.T.astype(BF16)
        v_ref[:, 2 * hh * HEAD_DIM:(2 * hh + 1) * HEAD_DIM] = pkv[:, n_kv + hh * HEAD_DIM:
                                                                  n_kv + (hh + 1) * HEAD_DIM].astype(BF16)
        v_ref[:, (2 * hh + 1) * HEAD_DIM:(2 * hh + 2) * HEAD_DIM] = jnp.ones((tile, HEAD_DIM), BF16)


def _kv(x, mod, mod_row, layer, w_qkv, k_g, cos, sin_signed, o, tile, with_rope, name):
    bsz, n, d = x.shape
    n_q = N_HEADS * HEAD_DIM
    n_kv = N_KV_HEADS * HEAD_DIM
    assert n_q % (2 * n_kv) == 0
    mod_map = (lambda b, i: (layer, b, 0, 0)) if mod_row is None else (lambda b, i: (layer, mod_row, 0, 0))
    return pl.pallas_call(
        functools.partial(_kv_kernel, with_rope=with_rope, tile=tile),
        grid=(bsz, n // tile),
        in_specs=[
            pl.BlockSpec((None, tile, d), lambda b, i: (b, i, 0)),
            pl.BlockSpec((None, None, N_MOD, d), mod_map),
            _resident((None, d, 2 * n_kv), lambda b, i: (o, 0, n_q // (2 * n_kv))),
            _resident((None, 1, HEAD_DIM), lambda b, i: (o, 0, 0)),
            pl.BlockSpec((tile, HEAD_DIM), lambda b, i: (i, 0)),
            pl.BlockSpec((tile, HEAD_DIM), lambda b, i: (i, 0)),
        ],
        out_specs=[pl.BlockSpec((None, n_kv, tile), lambda b, i: (b, 0, i)),
                   pl.BlockSpec((None, tile, 2 * n_kv), lambda b, i: (b, i, 0))],
        out_shape=[jax.ShapeDtypeStruct((bsz, n_kv, n), BF16),
                   jax.ShapeDtypeStruct((bsz, n, 2 * n_kv), BF16)],
        compiler_params=_params(2),
        name=name,
    )(x, mod, w_qkv, k_g, cos, sin_signed)


def _rope_tables(n):
    rows = n // GRID_W
    row = jnp.repeat(jnp.arange(rows, dtype=F32), GRID_W)
    col = jnp.tile(jnp.arange(GRID_W, dtype=F32), rows)
    n_freq = HEAD_DIM // 4
    inv = ROPE_THETA ** (-jnp.arange(n_freq, dtype=F32) / n_freq)
    ang_r = row[:, None] * inv
    ang_c = col[:, None] * inv
    cos = jnp.concatenate([jnp.cos(ang_r), jnp.cos(ang_r), jnp.cos(ang_c), jnp.cos(ang_c)], axis=-1)
    sin = jnp.concatenate([-jnp.sin(ang_r), jnp.sin(ang_r), -jnp.sin(ang_c), jnp.sin(ang_c)], axis=-1)
    return cos, sin


def _attn_kernel(x_ref, mod_ref, wq_ref, qg_ref, cos_ref, sin_ref, ktl_ref, vl_ref, ktc_ref, vc_ref,
                 wout_ref, lng_ref, lnb_ref, o_ref, o_scr, *, tile, alpha):
    x = x_ref[...]
    h = _modulated_bf16(x, mod_ref, 0)
    pq = jnp.dot(h, wq_ref[...], preferred_element_type=F32)
    cos = cos_ref[...]
    sin_signed = sin_ref[...]
    swap_lo = _swap_lo_mask(tile)
    q_scale = HEAD_DIM ** -0.5 * math.log2(math.e)
    for kvh in range(N_KV_HEADS):
        ksl = slice(kvh * HEAD_DIM, (kvh + 1) * HEAD_DIM)
        vsl = slice(2 * kvh * HEAD_DIM, (2 * kvh + 2) * HEAD_DIM)
        for g in range(Q_PER_KV):
            hsl = slice((kvh * Q_PER_KV + g) * HEAD_DIM, (kvh * Q_PER_KV + g + 1) * HEAD_DIM)
            qh = _rope_head(_rms_head(pq[:, hsl], qg_ref[...]), cos, sin_signed, swap_lo)
            qh = (qh * q_scale).astype(BF16)
            s_lat = jnp.dot(qh, ktl_ref[ksl, :], preferred_element_type=F32)
            s_ctx = jnp.dot(qh, ktc_ref[ksl, :], preferred_element_type=F32)
            m = jnp.maximum(jnp.max(s_lat, axis=-1, keepdims=True),
                            jnp.max(s_ctx, axis=-1, keepdims=True))
            p_lat = jnp.exp2(s_lat - m).astype(BF16)
            p_ctx = jnp.exp2(s_ctx - m).astype(BF16)
            o = (jnp.dot(p_lat, vl_ref[:, vsl], preferred_element_type=F32)
                 + jnp.dot(p_ctx, vc_ref[:, vsl], preferred_element_type=F32))
            o_scr[:, hsl] = (o[:, 0:HEAD_DIM] / o[:, HEAD_DIM:HEAD_DIM + 1]).astype(BF16)
    y = jnp.dot(o_scr[...], wout_ref[...], preferred_element_type=F32)
    gate = mod_ref[2:3, :]
    o_ref[...] = _layernorm(alpha * x + gate * y, lng_ref[0:1, :], lnb_ref[0:1, :])


def _attention(x, mod, layer, w_qkv, q_g, cos, sin_signed, kt_lat, v_lat, kt_ctx, v_ctx, w_out,
               ln_g, ln_b, o, tile, alpha):
    bsz, n, d = x.shape
    n_ctx = kt_ctx.shape[-1]
    n_q = N_HEADS * HEAD_DIM
    n_kv = N_KV_HEADS * HEAD_DIM
    return pl.pallas_call(
        functools.partial(_attn_kernel, tile=tile, alpha=alpha),
        grid=(bsz, n // tile),
        in_specs=[
            pl.BlockSpec((None, tile, d), lambda b, i: (b, i, 0)),
            pl.BlockSpec((None, None, N_MOD, d), lambda b, i: (layer, b, 0, 0)),
            _resident((None, d, n_q), lambda b, i: (o, 0, 0)),
            _resident((None, 1, HEAD_DIM), lambda b, i: (o, 0, 0)),
            pl.BlockSpec((tile, HEAD_DIM), lambda b, i: (i, 0)),
            pl.BlockSpec((tile, HEAD_DIM), lambda b, i: (i, 0)),
            pl.BlockSpec((None, n_kv, n), lambda b, i: (b, 0, 0)),
            pl.BlockSpec((None, n, 2 * n_kv), lambda b, i: (b, 0, 0)),
            pl.BlockSpec((None, n_kv, n_ctx), lambda b, i: (b, 0, 0)),
            pl.BlockSpec((None, n_ctx, 2 * n_kv), lambda b, i: (b, 0, 0)),
            _resident((None, n_q, d), lambda b, i: (o, 0, 0)),
            _resident((None, 2, d), lambda b, i: (layer, 0, 0)),
            _resident((None, 2, d), lambda b, i: (layer, 0, 0)),
        ],
        out_specs=pl.BlockSpec((None, tile, d), lambda b, i: (b, i, 0)),
        out_shape=jax.ShapeDtypeStruct((bsz, n, d), F32),
        scratch_shapes=[pltpu.VMEM((tile, n_q), BF16)],
        compiler_params=_params(2),
        name="attention",
    )(x, mod, w_qkv, q_g, cos, sin_signed, kt_lat, v_lat, kt_ctx, v_ctx, w_out, ln_g, ln_b)


def kernel(x, c, ctx, c_ctx, ada_w, ada_b, ln_g, ln_b, ffn_w_in, ffn_w_out, mix_w_in, conv_w,
           sgu_ln_g, sgu_ln_b, sgu_w, sgu_b, mix_w_out, attn_w_qkv, q_norm_g, k_norm_g, attn_w_out):
    bsz, seq, d = x.shape
    n_ctx = ctx.shape[1]
    depth = ada_w.shape[0]
    alpha = (2.0 * depth) ** 0.25
    assert seq % TILE_MIX == 0 and seq % TILE_KV == 0 and seq % TILE_ATTN == 0 and seq % TILE_FFN == 0
    assert n_ctx % CHUNK == 0 and seq % GRID_W == 0
    tile_ctx = n_ctx

    pad = (-(bsz + 1)) % MOD_ROWS_PAD
    c_all = jnp.concatenate([c, c_ctx[None, :], jnp.zeros((pad, d), F32)], axis=0)
    ctx_row = bsz
    mod = _ada_modulation(c_all, ada_w, ada_b)

    ffn_w_in_b = ffn_w_in.astype(BF16)
    ffn_w_out_b = ffn_w_out.astype(BF16)
    mix_w_in_b = mix_w_in.astype(BF16)
    mix_w_out_b = mix_w_out.astype(BF16)
    attn_w_qkv_b = attn_w_qkv.astype(BF16)
    attn_w_out_b = attn_w_out.astype(BF16)
    n_even, n_groups = sgu_w.shape[0], sgu_w.shape[1]
    cw = sgu_ln_g.shape[-1]
    sgu_wcat = jnp.transpose(sgu_w, (0, 2, 1, 3)).reshape(n_even, CHUNK, n_groups * CHUNK).astype(BF16)
    sgu_bias = jnp.repeat(jnp.transpose(sgu_b, (0, 2, 1)), cw // n_groups, axis=-1)
    sgu_ln_g3 = sgu_ln_g.reshape(n_even, 1, cw)
    sgu_ln_b3 = sgu_ln_b.reshape(n_even, 1, cw)
    q_g3 = q_norm_g.reshape(-1, 1, HEAD_DIM)
    k_g3 = k_norm_g.reshape(-1, 1, HEAD_DIM)
    cos, sin_signed = _rope_tables(seq)

    for l in range(depth):
        last = l == depth - 1
        if l % 2 == 0:
            e = l // 2
            mix_args = (conv_w, sgu_wcat, sgu_bias, mix_w_out_b, ln_g, ln_b, e)
            q_lat = _mix_in(x, mod, None, l, mix_w_in_b, sgu_ln_g3, sgu_ln_b3, e, TILE_MIX)
            x_mid = _mix_out(q_lat, x, mod, None, l, *mix_args, TILE_MIX, alpha)
            if not last:
                q_ctx = _mix_in(ctx, mod, ctx_row, l, mix_w_in_b, sgu_ln_g3, sgu_ln_b3, e, tile_ctx)
                ctx_mid = _mix_out(q_ctx, ctx, mod, ctx_row, l, *mix_args, tile_ctx, alpha)
        else:
            o = l // 2
            assert last, "an attention layer that also updates the context stream is not implemented"
            kt_lat, v_lat = _kv(x, mod, None, l, attn_w_qkv_b, k_g3, cos, sin_signed, o, TILE_KV, True,
                                "kv_latent")
            kt_ctx, v_ctx = _kv(ctx, mod, ctx_row, l, attn_w_qkv_b, k_g3, cos, sin_signed, o, tile_ctx,
                                False, "kv_context")
            x_mid = _attention(x, mod, l, attn_w_qkv_b, q_g3, cos, sin_signed, kt_lat, v_lat, kt_ctx,
                               v_ctx, attn_w_out_b, ln_g, ln_b, o, TILE_ATTN, alpha)
        x = _ffn(x_mid, mod, None, l, ffn_w_in_b, ffn_w_out_b, ln_g, ln_b, TILE_FFN, alpha)
        if not last:
            ctx = _ffn(ctx_mid, mod, ctx_row, l, ffn_w_in_b, ffn_w_out_b, ln_g, ln_b, tile_ctx, alpha)
    return x
```

```python
import functools
import math

import jax
import jax.numpy as jnp
from jax import lax
from jax.experimental import pallas as pl
from jax.experimental.pallas import tpu as pltpu

F32 = jnp.float32
BF16 = jnp.bfloat16

GRID_W = 64
N_HEADS = 8
N_KV_HEADS = 2
Q_PER_KV = N_HEADS // N_KV_HEADS
HEAD_DIM = 128
ROPE_THETA = 10000.0
CONV_TAPS = 3
SGU_GROUPS = 8
CHUNK = 128
LN_EPS = 1e-5
RMS_EPS = 1e-6
N_MOD = 6
MOD_ROWS_PAD = 8

V7X_VMEM_LIMIT_BYTES = 56 * 1024 * 1024
V7X_MXU_DIM = 256
BF16_SUBLANES = 16
V_ROWS = HEAD_DIM + BF16_SUBLANES

TILE_MIX = 512
TILE_FFN = 1024
TILE_KV = 512
TILE_ATTN = 1024
FFN_SUB_ROWS = 256
MIX_SUB_ROWS = 256
KV_SUB_ROWS = 128
ATTN_SUB_ROWS = 512
ATTN_SCORES_AHEAD = 1
FFN_HIDDEN_CHUNKS = 2


def _params(n_grid_dims):
    return pltpu.CompilerParams(
        dimension_semantics=("arbitrary",) * n_grid_dims,
        vmem_limit_bytes=V7X_VMEM_LIMIT_BYTES)


def _resident(shape, index_map):
    return pl.BlockSpec(shape, index_map, pipeline_mode=pl.Buffered(1))


def _layernorm(xf, g, b):
    mu = jnp.mean(xf, axis=-1, keepdims=True)
    xc = xf - mu
    var = jnp.mean(xc * xc, axis=-1, keepdims=True)
    return xc * lax.rsqrt(var + LN_EPS) * g + b


def _gelu(x):
    return 0.5 * x * (1.0 + lax.erf(x * (1.0 / math.sqrt(2.0))))


def _silu(x):
    return x * jax.nn.sigmoid(x)


def _modulated_bf16(x, mod_ref, shift_row):
    shift = mod_ref[shift_row:shift_row + 1, :]
    scale = mod_ref[shift_row + 1:shift_row + 2, :]
    return (x * (1.0 + scale) + shift).astype(BF16)


def _ada_kernel(c_ref, w_ref, b_ref, o_ref):
    s = _silu(c_ref[...])
    o_ref[...] = jnp.dot(s, w_ref[...], preferred_element_type=F32,
                         precision=lax.Precision.HIGHEST) + b_ref[...]


def _ada_modulation(c_all, ada_w, ada_b):
    depth, d, n6 = ada_w.shape
    rows = c_all.shape[0]
    tn = d
    out = pl.pallas_call(
        _ada_kernel,
        grid=(depth, n6 // tn),
        in_specs=[
            pl.BlockSpec((rows, d), lambda l, j: (0, 0)),
            pl.BlockSpec((None, d, tn), lambda l, j: (l, 0, j)),
            pl.BlockSpec((None, 1, tn), lambda l, j: (l, 0, j)),
        ],
        out_specs=pl.BlockSpec((None, rows, tn), lambda l, j: (l, 0, j)),
        out_shape=jax.ShapeDtypeStruct((depth, rows, n6), F32),
        compiler_params=_params(2),
        name="ada_modulation",
    )(c_all, ada_w, ada_b.reshape(depth, 1, n6))
    return out.reshape(depth, rows, N_MOD, d)


def _mix_in_kernel(x_ref, mod_ref, w_ref, lng_ref, lnb_ref, o_ref, *, cw, sub):
    for r in range(x_ref.shape[0] // sub):
        rows = slice(r * sub, (r + 1) * sub)
        h = _modulated_bf16(x_ref[rows, :], mod_ref, 0)
        p = jnp.dot(h, w_ref[...], preferred_element_type=F32)
        o_ref[rows, 0:cw] = p[:, 0:cw].astype(BF16)
        o_ref[rows, cw:2 * cw] = (p[:, cw:2 * cw] * p[:, 2 * cw:3 * cw]).astype(BF16)
        o_ref[rows, 2 * cw:3 * cw] = _gelu(p[:, 3 * cw:4 * cw]).astype(BF16)
        v = _layernorm(_gelu(p[:, 4 * cw:5 * cw]), lng_ref[...], lnb_ref[...])
        o_ref[rows, 3 * cw:4 * cw] = v.astype(BF16)


def _mix_in(x, mod, mod_row, layer, w_in, sgu_ln_g, sgu_ln_b, e, tile):
    bsz, n, d = x.shape
    cw = w_in.shape[-1] // 5
    mod_map = (lambda b, i: (layer, b, 0, 0)) if mod_row is None else (lambda b, i: (layer, mod_row, 0, 0))
    return pl.pallas_call(
        functools.partial(_mix_in_kernel, cw=cw, sub=min(MIX_SUB_ROWS, tile)),
        grid=(bsz, n // tile),
        in_specs=[
            pl.BlockSpec((None, tile, d), lambda b, i: (b, i, 0)),
            pl.BlockSpec((None, None, N_MOD, d), mod_map),
            _resident((None, d, 5 * cw), lambda b, i: (e, 0, 0)),
            _resident((None, 1, cw), lambda b, i: (e, 0, 0)),
            _resident((None, 1, cw), lambda b, i: (e, 0, 0)),
        ],
        out_specs=pl.BlockSpec((None, tile, 4 * cw), lambda b, i: (b, i, 0)),
        out_shape=jax.ShapeDtypeStruct((bsz, n, 4 * cw), BF16),
        compiler_params=_params(2),
        name="mix_in",
    )(x, mod, w_in, sgu_ln_g, sgu_ln_b)


def _mix_out_kernel(q_ref, zprev_ref, znext_ref, x_ref, mod_ref, convw_ref, wcat_ref, diag_ref, sb_ref,
                    wout_ref, lng_ref, lnb_ref, o_ref, y_scr, *, cw, tile, alpha):
    i = pl.program_id(1)
    n_tiles = pl.num_programs(1)

    z = q_ref[:, cw:2 * cw].astype(F32)
    rows = lax.broadcasted_iota(jnp.int32, (tile, cw), 0)
    prev_row = jnp.where(i > 0, zprev_ref[BF16_SUBLANES - 1:BF16_SUBLANES, :].astype(F32), 0.0)
    next_row = jnp.where(i < n_tiles - 1, znext_ref[0:1, :].astype(F32), 0.0)
    z_before = jnp.where(rows == 0, prev_row, pltpu.roll(z, 1, axis=0))
    z_after = jnp.where(rows == tile - 1, next_row, pltpu.roll(z, tile - 1, axis=0))
    zc = convw_ref[0:1, :] * z_before + convw_ref[1:2, :] * z + convw_ref[2:3, :] * z_after
    y_scr[:, 0:cw] = (q_ref[:, 0:cw].astype(F32) * zc).astype(BF16)

    gpt = diag_ref.shape[0] // CHUNK
    for c in range(tile // CHUNK):
        sl = slice(c * CHUNK, (c + 1) * CHUNK)
        for j in range(cw // V7X_MXU_DIM):
            lanes = slice(3 * cw + j * V7X_MXU_DIM, 3 * cw + (j + 1) * V7X_MXU_DIM)
            v = q_ref[sl, lanes]
            vd = jnp.concatenate([v] * gpt, axis=0) * diag_ref[...]
            w = wcat_ref[:, j * gpt * CHUNK:(j + 1) * gpt * CHUNK]
            out = slice(j * V7X_MXU_DIM, (j + 1) * V7X_MXU_DIM)
            s = jnp.dot(w, vd, preferred_element_type=F32) + sb_ref[:, out]
            u = q_ref[sl, 2 * cw + j * V7X_MXU_DIM:2 * cw + (j + 1) * V7X_MXU_DIM]
            y_scr[sl, cw + j * V7X_MXU_DIM:cw + (j + 1) * V7X_MXU_DIM] = (u.astype(F32) * s).astype(BF16)

    y = jnp.dot(y_scr[...], wout_ref[...], preferred_element_type=F32)
    gate = mod_ref[2:3, :]
    o_ref[...] = _layernorm(alpha * x_ref[...] + gate * y, lng_ref[0:1, :], lnb_ref[0:1, :])


def _mix_out(q, x, mod, mod_row, layer, conv_w, wcat, sbias, w_out, ln_g, ln_b, e, tile, alpha):
    bsz, n, d = x.shape
    cw = q.shape[-1] // 4
    n_groups = wcat.shape[-1] // CHUNK
    gpt = V7X_MXU_DIM // (cw // n_groups)
    diag = jnp.kron(jnp.eye(gpt, dtype=BF16), jnp.ones((CHUNK, V7X_MXU_DIM // gpt), BF16))
    halo = BF16_SUBLANES
    per_tile = tile // halo
    n_halo_blocks = n // halo
    mod_map = (lambda b, i: (layer, b, 0, 0)) if mod_row is None else (lambda b, i: (layer, mod_row, 0, 0))
    return pl.pallas_call(
        functools.partial(_mix_out_kernel, cw=cw, tile=tile, alpha=alpha),
        grid=(bsz, n // tile),
        in_specs=[
            pl.BlockSpec((None, tile, 4 * cw), lambda b, i: (b, i, 0)),
            pl.BlockSpec((None, halo, cw), lambda b, i: (b, jnp.maximum(i * per_tile - 1, 0), 1)),
            pl.BlockSpec((None, halo, cw),
                         lambda b, i: (b, jnp.minimum((i + 1) * per_tile, n_halo_blocks - 1), 1)),
            pl.BlockSpec((None, tile, d), lambda b, i: (b, i, 0)),
            pl.BlockSpec((None, None, N_MOD, d), mod_map),
            _resident((None, CONV_TAPS, cw), lambda b, i: (e, 0, 0)),
            _resident((None, CHUNK, wcat.shape[-1]), lambda b, i: (e, 0, 0)),
            _resident(diag.shape, lambda b, i: (0, 0)),
            _resident((None, CHUNK, cw), lambda b, i: (e, 0, 0)),
            _resident((None, 2 * cw, d), lambda b, i: (e, 0, 0)),
            _resident((None, 2, d), lambda b, i: (layer, 0, 0)),
            _resident((None, 2, d), lambda b, i: (layer, 0, 0)),
        ],
        out_specs=pl.BlockSpec((None, tile, d), lambda b, i: (b, i, 0)),
        out_shape=jax.ShapeDtypeStruct((bsz, n, d), F32),
        scratch_shapes=[pltpu.VMEM((tile, 2 * cw), BF16)],
        compiler_params=_params(2),
        name="mix_out",
    )(q, q, q, x, mod, conv_w, wcat, diag, sbias, w_out, ln_g, ln_b)


def _hidden_chunks(hidden, n_chunks):
    tiles = hidden // V7X_MXU_DIM
    assert tiles * V7X_MXU_DIM == hidden
    bounds = [V7X_MXU_DIM * ((tiles * c + n_chunks - 1) // n_chunks) for c in range(n_chunks + 1)]
    return tuple(zip(bounds[:-1], bounds[1:]))


def _ffn_kernel(x_ref, mod_ref, win_ref, wout_ref, lng_ref, lnb_ref, o_ref, *, hidden, chunks, sub, alpha):
    gate = mod_ref[5:6, :]
    n_sub = x_ref.shape[0] // sub
    h = {}
    y = {}

    def expand(r, c0, c1):
        if r not in h:
            h[r] = _modulated_bf16(x_ref[r * sub:(r + 1) * sub, :], mod_ref, 3)
        return (jnp.dot(h[r], win_ref[:, c0:c1], preferred_element_type=F32),
                jnp.dot(h[r], win_ref[:, hidden + c0:hidden + c1], preferred_element_type=F32))

    def contract(gt_up, r, c0, c1):
        gt, up = gt_up
        a = (_silu(gt) * up).astype(BF16)
        part = jnp.dot(a, wout_ref[c0:c1, :], preferred_element_type=F32)
        y[r] = part if r not in y else y[r] + part
        if c1 == hidden:
            rows = slice(r * sub, (r + 1) * sub)
            o_ref[rows, :] = _layernorm(alpha * x_ref[rows, :] + gate * y.pop(r),
                                        lng_ref[1:2, :], lnb_ref[1:2, :])

    units = [(r, c0, c1) for r in range(n_sub) for c0, c1 in chunks]
    pending = expand(*units[0])
    for u, unit in enumerate(units):
        current = pending
        if u + 1 < len(units):
            pending = expand(*units[u + 1])
        contract(current, *unit)


def _ffn(x, mod, mod_row, layer, w_in, w_out, ln_g, ln_b, tile, alpha):
    bsz, n, d = x.shape
    hidden = w_out.shape[1]
    mod_map = (lambda b, i: (layer, b, 0, 0)) if mod_row is None else (lambda b, i: (layer, mod_row, 0, 0))
    return pl.pallas_call(
        functools.partial(_ffn_kernel, hidden=hidden, chunks=_hidden_chunks(hidden, FFN_HIDDEN_CHUNKS),
                          sub=min(FFN_SUB_ROWS, tile), alpha=alpha),
        grid=(bsz, n // tile),
        in_specs=[
            pl.BlockSpec((None, tile, d), lambda b, i: (b, i, 0)),
            pl.BlockSpec((None, None, N_MOD, d), mod_map),
            _resident((None, d, 2 * hidden), lambda b, i: (layer, 0, 0)),
            _resident((None, hidden, d), lambda b, i: (layer, 0, 0)),
            _resident((None, 2, d), lambda b, i: (layer, 0, 0)),
            _resident((None, 2, d), lambda b, i: (layer, 0, 0)),
        ],
        out_specs=pl.BlockSpec((None, tile, d), lambda b, i: (b, i, 0)),
        out_shape=jax.ShapeDtypeStruct((bsz, n, d), F32),
        compiler_params=_params(2),
        name="ffn",
    )(x, mod, w_in, w_out, ln_g, ln_b)


def _rms_head(xh, g):
    return xh * lax.rsqrt(jnp.mean(xh * xh, axis=-1, keepdims=True) + RMS_EPS) * g


def _rope_head(xh, cos, sin_signed, swap_lo):
    partner = jnp.where(swap_lo, pltpu.roll(xh, HEAD_DIM - HEAD_DIM // 4, axis=1),
                        pltpu.roll(xh, HEAD_DIM // 4, axis=1))
    return xh * cos + partner * sin_signed


def _swap_lo_mask(rows):
    lane = lax.broadcasted_iota(jnp.int32, (rows, HEAD_DIM), 1)
    return (lane % (HEAD_DIM // 2)) < (HEAD_DIM // 4)


def _kv_kernel(x_ref, mod_ref, w_ref, kg_ref, cos_ref, sin_ref, k_ref, vt_ref, *, with_rope, tile):
    n_kv = N_KV_HEADS * HEAD_DIM
    sub = min(KV_SUB_ROWS, tile)
    for r in range(tile // sub):
        rows = slice(r * sub, (r + 1) * sub)
        h = _modulated_bf16(x_ref[rows, :], mod_ref, 0)
        pkv = jnp.dot(h, w_ref[...], preferred_element_type=F32)
        for hh in range(N_KV_HEADS):
            sl = slice(hh * HEAD_DIM, (hh + 1) * HEAD_DIM)
            kh = _rms_head(pkv[:, sl], kg_ref[...])
            if with_rope:
                kh = _rope_head(kh, cos_ref[rows, :], sin_ref[rows, :], _swap_lo_mask(sub))
            k_ref[rows, sl] = kh.astype(BF16)
            vh = pkv[:, n_kv + hh * HEAD_DIM:n_kv + (hh + 1) * HEAD_DIM]
            vt_ref[hh, 0:HEAD_DIM, rows] = vh.T.astype(BF16)
            vt_ref[hh, HEAD_DIM:HEAD_DIM + BF16_SUBLANES, rows] = jnp.ones((BF16_SUBLANES, sub), BF16)


def _kv(x, mod, mod_row, layer, w_qkv, k_g, cos, sin_signed, o, tile, with_rope, name):
    bsz, n, d = x.shape
    n_q = N_HEADS * HEAD_DIM
    n_kv = N_KV_HEADS * HEAD_DIM
    assert n_q % (2 * n_kv) == 0
    mod_map = (lambda b, i: (layer, b, 0, 0)) if mod_row is None else (lambda b, i: (layer, mod_row, 0, 0))
    return pl.pallas_call(
        functools.partial(_kv_kernel, with_rope=with_rope, tile=tile),
        grid=(bsz, n // tile),
        in_specs=[
            pl.BlockSpec((None, tile, d), lambda b, i: (b, i, 0)),
            pl.BlockSpec((None, None, N_MOD, d), mod_map),
            _resident((None, d, 2 * n_kv), lambda b, i: (o, 0, n_q // (2 * n_kv))),
            _resident((None, 1, HEAD_DIM), lambda b, i: (o, 0, 0)),
            pl.BlockSpec((tile, HEAD_DIM), lambda b, i: (i, 0)),
            pl.BlockSpec((tile, HEAD_DIM), lambda b, i: (i, 0)),
        ],
        out_specs=[pl.BlockSpec((None, tile, n_kv), lambda b, i: (b, i, 0)),
                   pl.BlockSpec((None, N_KV_HEADS, V_ROWS, tile), lambda b, i: (b, 0, 0, i))],
        out_shape=[jax.ShapeDtypeStruct((bsz, n, n_kv), BF16),
                   jax.ShapeDtypeStruct((bsz, N_KV_HEADS, V_ROWS, n), BF16)],
        compiler_params=_params(2),
        name=name,
    )(x, mod, w_qkv, k_g, cos, sin_signed)


def _rope_tables(n):
    rows = n // GRID_W
    row = jnp.repeat(jnp.arange(rows, dtype=F32), GRID_W)
    col = jnp.tile(jnp.arange(GRID_W, dtype=F32), rows)
    n_freq = HEAD_DIM // 4
    inv = ROPE_THETA ** (-jnp.arange(n_freq, dtype=F32) / n_freq)
    ang_r = row[:, None] * inv
    ang_c = col[:, None] * inv
    cos = jnp.concatenate([jnp.cos(ang_r), jnp.cos(ang_r), jnp.cos(ang_c), jnp.cos(ang_c)], axis=-1)
    sin = jnp.concatenate([-jnp.sin(ang_r), jnp.sin(ang_r), -jnp.sin(ang_c), jnp.sin(ang_c)], axis=-1)
    return cos, sin


def _attn_kernel(x_ref, mod_ref, wq_ref, qg_ref, cos_ref, sin_ref, kl_ref, vtl_ref, kc_ref, vtc_ref,
                 wout_ref, lng_ref, lnb_ref, o_ref, o_scr, *, tile, alpha):
    n_sub = o_scr.shape[0]
    sub = tile // n_sub
    swap_lo = _swap_lo_mask(sub)
    q_scale = HEAD_DIM ** -0.5 * math.log2(math.e)
    gate = mod_ref[2:3, :]

    def q_proj(r):
        h = _modulated_bf16(x_ref[r * sub:(r + 1) * sub, :], mod_ref, 0)
        return jnp.dot(h, wq_ref[...], preferred_element_type=F32)

    def scores(pq, r, hh):
        rows = slice(r * sub, (r + 1) * sub)
        ksl = slice(hh // Q_PER_KV * HEAD_DIM, (hh // Q_PER_KV + 1) * HEAD_DIM)
        qh = _rope_head(_rms_head(pq[:, hh * HEAD_DIM:(hh + 1) * HEAD_DIM], qg_ref[...]),
                        cos_ref[rows, :], sin_ref[rows, :], swap_lo)
        qt = (qh * q_scale).T.astype(BF16)
        return (jnp.dot(kl_ref[:, ksl], qt, preferred_element_type=F32),
                jnp.dot(kc_ref[:, ksl], qt, preferred_element_type=F32))

    def attend(s, r, hh):
        s_lat, s_ctx = s
        m = jnp.maximum(jnp.max(s_lat, axis=0, keepdims=True),
                        jnp.max(s_ctx, axis=0, keepdims=True))
        p_lat = jnp.exp2(s_lat - m).astype(BF16)
        p_ctx = jnp.exp2(s_ctx - m).astype(BF16)
        ot = (jnp.dot(vtl_ref[hh // Q_PER_KV], p_lat, preferred_element_type=F32)
              + jnp.dot(vtc_ref[hh // Q_PER_KV], p_ctx, preferred_element_type=F32))
        o = ot[0:HEAD_DIM, :] / ot[HEAD_DIM:HEAD_DIM + 1, :]
        o_scr[r, :, hh * HEAD_DIM:(hh + 1) * HEAD_DIM] = o.T.astype(BF16)

    def out_proj(r):
        rows = slice(r * sub, (r + 1) * sub)
        y = jnp.dot(o_scr[r], wout_ref[...], preferred_element_type=F32)
        o_ref[rows, :] = _layernorm(alpha * x_ref[rows, :] + gate * y, lng_ref[0:1, :], lnb_ref[0:1, :])

    units = [(r, hh) for r in range(n_sub) for hh in range(N_HEADS)]
    pq = {0: q_proj(0)}
    ahead = [scores(pq[r], r, hh) for r, hh in units[:ATTN_SCORES_AHEAD]]
    for u, (r, hh) in enumerate(units):
        if hh == N_HEADS // 2 and r + 1 < n_sub:
            pq[r + 1] = q_proj(r + 1)
        if u + ATTN_SCORES_AHEAD < len(units):
            r_next, hh_next = units[u + ATTN_SCORES_AHEAD]
            ahead.append(scores(pq[r_next], r_next, hh_next))
        attend(ahead.pop(0), r, hh)
        if hh == 1 and r > 0:
            out_proj(r - 1)
    out_proj(n_sub - 1)


def _attention(x, mod, layer, w_qkv, q_g, cos, sin_signed, k_lat, vt_lat, k_ctx, vt_ctx, w_out,
               ln_g, ln_b, o, tile, alpha):
    bsz, n, d = x.shape
    n_ctx = k_ctx.shape[1]
    n_q = N_HEADS * HEAD_DIM
    n_kv = N_KV_HEADS * HEAD_DIM
    return pl.pallas_call(
        functools.partial(_attn_kernel, tile=tile, alpha=alpha),
        grid=(bsz, n // tile),
        in_specs=[
            pl.BlockSpec((None, tile, d), lambda b, i: (b, i, 0)),
            pl.BlockSpec((None, None, N_MOD, d), lambda b, i: (layer, b, 0, 0)),
            _resident((None, d, n_q), lambda b, i: (o, 0, 0)),
            _resident((None, 1, HEAD_DIM), lambda b, i: (o, 0, 0)),
            pl.BlockSpec((tile, HEAD_DIM), lambda b, i: (i, 0)),
            pl.BlockSpec((tile, HEAD_DIM), lambda b, i: (i, 0)),
            pl.BlockSpec((None, n, n_kv), lambda b, i: (b, 0, 0)),
            pl.BlockSpec((None, N_KV_HEADS, V_ROWS, n), lambda b, i: (b, 0, 0, 0)),
            pl.BlockSpec((None, n_ctx, n_kv), lambda b, i: (b, 0, 0)),
            pl.BlockSpec((None, N_KV_HEADS, V_ROWS, n_ctx), lambda b, i: (b, 0, 0, 0)),
            _resident((None, n_q, d), lambda b, i: (o, 0, 0)),
            _resident((None, 2, d), lambda b, i: (layer, 0, 0)),
            _resident((None, 2, d), lambda b, i: (layer, 0, 0)),
        ],
        out_specs=pl.BlockSpec((None, tile, d), lambda b, i: (b, i, 0)),
        out_shape=jax.ShapeDtypeStruct((bsz, n, d), F32),
        scratch_shapes=[pltpu.VMEM((tile // ATTN_SUB_ROWS, ATTN_SUB_ROWS, n_q), BF16)],
        compiler_params=_params(2),
        name="attention",
    )(x, mod, w_qkv, q_g, cos, sin_signed, k_lat, vt_lat, k_ctx, vt_ctx, w_out, ln_g, ln_b)


def kernel(x, c, ctx, c_ctx, ada_w, ada_b, ln_g, ln_b, ffn_w_in, ffn_w_out, mix_w_in, conv_w,
           sgu_ln_g, sgu_ln_b, sgu_w, sgu_b, mix_w_out, attn_w_qkv, q_norm_g, k_norm_g, attn_w_out):
    bsz, seq, d = x.shape
    n_ctx = ctx.shape[1]
    depth = ada_w.shape[0]
    alpha = (2.0 * depth) ** 0.25
    assert seq % TILE_MIX == 0 and seq % TILE_KV == 0 and seq % TILE_ATTN == 0 and seq % TILE_FFN == 0
    assert n_ctx % CHUNK == 0 and seq % GRID_W == 0
    tile_ctx = n_ctx

    pad = (-(bsz + 1)) % MOD_ROWS_PAD
    c_all = jnp.concatenate([c, c_ctx[None, :], jnp.zeros((pad, d), F32)], axis=0)
    ctx_row = bsz
    mod = _ada_modulation(c_all, ada_w, ada_b)

    ffn_w_in_b = ffn_w_in.astype(BF16)
    ffn_w_out_b = ffn_w_out.astype(BF16)
    mix_w_in_b = mix_w_in.astype(BF16)
    mix_w_out_b = mix_w_out.astype(BF16)
    attn_w_qkv_b = attn_w_qkv.astype(BF16)
    attn_w_out_b = attn_w_out.astype(BF16)
    n_even, n_groups = sgu_w.shape[0], sgu_w.shape[1]
    cw = sgu_ln_g.shape[-1]
    sgu_wcat = jnp.transpose(sgu_w, (0, 2, 1, 3)).reshape(n_even, CHUNK, n_groups * CHUNK).astype(BF16)
    sgu_bias = jnp.repeat(jnp.transpose(sgu_b, (0, 2, 1)), cw // n_groups, axis=-1)
    sgu_ln_g3 = sgu_ln_g.reshape(n_even, 1, cw)
    sgu_ln_b3 = sgu_ln_b.reshape(n_even, 1, cw)
    q_g3 = q_norm_g.reshape(-1, 1, HEAD_DIM)
    k_g3 = k_norm_g.reshape(-1, 1, HEAD_DIM)
    cos, sin_signed = _rope_tables(seq)

    for l in range(depth):
        last = l == depth - 1
        if l % 2 == 0:
            e = l // 2
            mix_args = (conv_w, sgu_wcat, sgu_bias, mix_w_out_b, ln_g, ln_b, e)
            q_lat = _mix_in(x, mod, None, l, mix_w_in_b, sgu_ln_g3, sgu_ln_b3, e, TILE_MIX)
            x_mid = _mix_out(q_lat, x, mod, None, l, *mix_args, TILE_MIX, alpha)
            if not last:
                q_ctx = _mix_in(ctx, mod, ctx_row, l, mix_w_in_b, sgu_ln_g3, sgu_ln_b3, e, tile_ctx)
                ctx_mid = _mix_out(q_ctx, ctx, mod, ctx_row, l, *mix_args, tile_ctx, alpha)
        else:
            o = l // 2
            assert last, "an attention layer that also updates the context stream is not implemented"
            k_lat, vt_lat = _kv(x, mod, None, l, attn_w_qkv_b, k_g3, cos, sin_signed, o, TILE_KV, True,
                                "kv_latent")
            k_ctx, vt_ctx = _kv(ctx, mod, ctx_row, l, attn_w_qkv_b, k_g3, cos, sin_signed, o, tile_ctx,
                                False, "kv_context")
            x_mid = _attention(x, mod, l, attn_w_qkv_b, q_g3, cos, sin_signed, k_lat, vt_lat, k_ctx,
                               vt_ctx, attn_w_out_b, ln_g, ln_b, o, TILE_ATTN, alpha)
        x = _ffn(x_mid, mod, None, l, ffn_w_in_b, ffn_w_out_b, ln_g, ln_b, TILE_FFN, alpha)
        if not last:
            ctx = _ffn(ctx_mid, mod, ctx_row, l, ffn_w_in_b, ffn_w_out_b, ln_g, ln_b, tile_ctx, alpha)
    return x
```

```python
import functools
import math

import jax
import jax.numpy as jnp
from jax import lax
from jax.experimental import pallas as pl
from jax.experimental.pallas import tpu as pltpu

F32 = jnp.float32
BF16 = jnp.bfloat16

GRID_W = 64
N_HEADS = 8
N_KV_HEADS = 2
Q_PER_KV = N_HEADS // N_KV_HEADS
HEAD_DIM = 128
ROPE_THETA = 10000.0
CONV_TAPS = 3
SGU_GROUPS = 8
CHUNK = 128
LN_EPS = 1e-5
RMS_EPS = 1e-6
N_MOD = 6
MOD_ROWS_PAD = 8

V7X_VMEM_LIMIT_BYTES = 56 * 1024 * 1024
V7X_MXU_DIM = 256
BF16_SUBLANES = 16

TILE_MIX_IN = 1024
TILE_MIX_OUT = 512
TILE_FFN = 1024
TILE_ATTN = 1024
FFN_SUB_ROWS = 256
MIX_SUB_ROWS = 256
ATTN_SUB_ROWS = 512
ATTN_SCORES_AHEAD = 1
FFN_HIDDEN_CHUNKS = 2


def _params(n_grid_dims):
    return pltpu.CompilerParams(
        dimension_semantics=("arbitrary",) * n_grid_dims,
        vmem_limit_bytes=V7X_VMEM_LIMIT_BYTES)


def _resident(shape, index_map):
    return pl.BlockSpec(shape, index_map, pipeline_mode=pl.Buffered(1))


def _layernorm(xf, g, b):
    mu = jnp.mean(xf, axis=-1, keepdims=True)
    xc = xf - mu
    var = jnp.mean(xc * xc, axis=-1, keepdims=True)
    return xc * lax.rsqrt(var + LN_EPS) * g + b


def _gelu(x):
    return 0.5 * x * (1.0 + lax.erf(x * (1.0 / math.sqrt(2.0))))


def _silu(x):
    return x * jax.nn.sigmoid(x)


def _modulated_bf16(x, mod_ref, shift_row):
    shift = mod_ref[shift_row:shift_row + 1, :]
    scale = mod_ref[shift_row + 1:shift_row + 2, :]
    return (x * (1.0 + scale) + shift).astype(BF16)


def _ada_kernel(c_ref, w_ref, b_ref, o_ref):
    s = _silu(c_ref[...])
    o_ref[...] = jnp.dot(s, w_ref[...], preferred_element_type=F32,
                         precision=lax.Precision.HIGHEST) + b_ref[...]


def _ada_modulation(c_all, ada_w, ada_b):
    depth, d, n6 = ada_w.shape
    rows = c_all.shape[0]
    tn = d
    out = pl.pallas_call(
        _ada_kernel,
        grid=(depth, n6 // tn),
        in_specs=[
            pl.BlockSpec((rows, d), lambda l, j: (0, 0)),
            pl.BlockSpec((None, d, tn), lambda l, j: (l, 0, j)),
            pl.BlockSpec((None, 1, tn), lambda l, j: (l, 0, j)),
        ],
        out_specs=pl.BlockSpec((None, rows, tn), lambda l, j: (l, 0, j)),
        out_shape=jax.ShapeDtypeStruct((depth, rows, n6), F32),
        compiler_params=_params(2),
        name="ada_modulation",
    )(c_all, ada_w, ada_b.reshape(depth, 1, n6))
    return out.reshape(depth, rows, N_MOD, d)


def _mix_in_kernel(x_ref, mod_ref, w_ref, lng_ref, lnb_ref, o_ref, *, cw, sub):
    for r in range(x_ref.shape[0] // sub):
        rows = slice(r * sub, (r + 1) * sub)
        h = _modulated_bf16(x_ref[rows, :], mod_ref, 0)
        p = jnp.dot(h, w_ref[...], preferred_element_type=F32)
        o_ref[rows, 0:cw] = p[:, 0:cw].astype(BF16)
        o_ref[rows, cw:2 * cw] = (p[:, cw:2 * cw] * p[:, 2 * cw:3 * cw]).astype(BF16)
        o_ref[rows, 2 * cw:3 * cw] = _gelu(p[:, 3 * cw:4 * cw]).astype(BF16)
        v = _layernorm(_gelu(p[:, 4 * cw:5 * cw]), lng_ref[...], lnb_ref[...])
        o_ref[rows, 3 * cw:4 * cw] = v.astype(BF16)


def _mix_in(x, mod, mod_row, layer, w_in, sgu_ln_g, sgu_ln_b, e, tile):
    bsz, n, d = x.shape
    cw = w_in.shape[-1] // 5
    mod_map = (lambda b, i: (layer, b, 0, 0)) if mod_row is None else (lambda b, i: (layer, mod_row, 0, 0))
    return pl.pallas_call(
        functools.partial(_mix_in_kernel, cw=cw, sub=min(MIX_SUB_ROWS, tile)),
        grid=(bsz, n // tile),
        in_specs=[
            pl.BlockSpec((None, tile, d), lambda b, i: (b, i, 0)),
            pl.BlockSpec((None, None, N_MOD, d), mod_map),
            _resident((None, d, 5 * cw), lambda b, i: (e, 0, 0)),
            _resident((None, 1, cw), lambda b, i: (e, 0, 0)),
            _resident((None, 1, cw), lambda b, i: (e, 0, 0)),
        ],
        out_specs=pl.BlockSpec((None, tile, 4 * cw), lambda b, i: (b, i, 0)),
        out_shape=jax.ShapeDtypeStruct((bsz, n, 4 * cw), BF16),
        compiler_params=_params(2),
        name="mix_in",
    )(x, mod, w_in, sgu_ln_g, sgu_ln_b)


def _mix_out_kernel(q_ref, zprev_ref, znext_ref, x_ref, mod_ref, convw_ref, wcat_ref, diag_ref, sb_ref,
                    wout_ref, lng_ref, lnb_ref, o_ref, y_scr, *, cw, tile, alpha):
    i = pl.program_id(1)
    n_tiles = pl.num_programs(1)

    z = q_ref[:, cw:2 * cw].astype(F32)
    rows = lax.broadcasted_iota(jnp.int32, (tile, cw), 0)
    prev_row = jnp.where(i > 0, zprev_ref[BF16_SUBLANES - 1:BF16_SUBLANES, :].astype(F32), 0.0)
    next_row = jnp.where(i < n_tiles - 1, znext_ref[0:1, :].astype(F32), 0.0)
    z_before = jnp.where(rows == 0, prev_row, pltpu.roll(z, 1, axis=0))
    z_after = jnp.where(rows == tile - 1, next_row, pltpu.roll(z, tile - 1, axis=0))
    zc = convw_ref[0:1, :] * z_before + convw_ref[1:2, :] * z + convw_ref[2:3, :] * z_after
    y_scr[:, 0:cw] = (q_ref[:, 0:cw].astype(F32) * zc).astype(BF16)

    gpt = diag_ref.shape[0] // CHUNK
    for c in range(tile // CHUNK):
        sl = slice(c * CHUNK, (c + 1) * CHUNK)
        for j in range(cw // V7X_MXU_DIM):
            lanes = slice(3 * cw + j * V7X_MXU_DIM, 3 * cw + (j + 1) * V7X_MXU_DIM)
            v = q_ref[sl, lanes]
            vd = jnp.concatenate([v] * gpt, axis=0) * diag_ref[...]
            w = wcat_ref[:, j * gpt * CHUNK:(j + 1) * gpt * CHUNK]
            out = slice(j * V7X_MXU_DIM, (j + 1) * V7X_MXU_DIM)
            s = jnp.dot(w, vd, preferred_element_type=F32) + sb_ref[:, out]
            u = q_ref[sl, 2 * cw + j * V7X_MXU_DIM:2 * cw + (j + 1) * V7X_MXU_DIM]
            y_scr[sl, cw + j * V7X_MXU_DIM:cw + (j + 1) * V7X_MXU_DIM] = (u.astype(F32) * s).astype(BF16)

    y = jnp.dot(y_scr[...], wout_ref[...], preferred_element_type=F32)
    gate = mod_ref[2:3, :]
    o_ref[...] = _layernorm(alpha * x_ref[...] + gate * y, lng_ref[0:1, :], lnb_ref[0:1, :])


def _mix_out(q, x, mod, mod_row, layer, conv_w, wcat, sbias, w_out, ln_g, ln_b, e, tile, alpha):
    bsz, n, d = x.shape
    cw = q.shape[-1] // 4
    n_groups = wcat.shape[-1] // CHUNK
    gpt = V7X_MXU_DIM // (cw // n_groups)
    diag = jnp.kron(jnp.eye(gpt, dtype=BF16), jnp.ones((CHUNK, V7X_MXU_DIM // gpt), BF16))
    halo = BF16_SUBLANES
    per_tile = tile // halo
    n_halo_blocks = n // halo
    mod_map = (lambda b, i: (layer, b, 0, 0)) if mod_row is None else (lambda b, i: (layer, mod_row, 0, 0))
    return pl.pallas_call(
        functools.partial(_mix_out_kernel, cw=cw, tile=tile, alpha=alpha),
        grid=(bsz, n // tile),
        in_specs=[
            pl.BlockSpec((None, tile, 4 * cw), lambda b, i: (b, i, 0)),
            pl.BlockSpec((None, halo, cw), lambda b, i: (b, jnp.maximum(i * per_tile - 1, 0), 1)),
            pl.BlockSpec((None, halo, cw),
                         lambda b, i: (b, jnp.minimum((i + 1) * per_tile, n_halo_blocks - 1), 1)),
            pl.BlockSpec((None, tile, d), lambda b, i: (b, i, 0)),
            pl.BlockSpec((None, None, N_MOD, d), mod_map),
            _resident((None, CONV_TAPS, cw), lambda b, i: (e, 0, 0)),
            _resident((None, CHUNK, wcat.shape[-1]), lambda b, i: (e, 0, 0)),
            _resident(diag.shape, lambda b, i: (0, 0)),
            _resident((None, CHUNK, cw), lambda b, i: (e, 0, 0)),
            _resident((None, 2 * cw, d), lambda b, i: (e, 0, 0)),
            _resident((None, 2, d), lambda b, i: (layer, 0, 0)),
            _resident((None, 2, d), lambda b, i: (layer, 0, 0)),
        ],
        out_specs=pl.BlockSpec((None, tile, d), lambda b, i: (b, i, 0)),
        out_shape=jax.ShapeDtypeStruct((bsz, n, d), F32),
        scratch_shapes=[pltpu.VMEM((tile, 2 * cw), BF16)],
        compiler_params=_params(2),
        name="mix_out",
    )(q, q, q, x, mod, conv_w, wcat, diag, sbias, w_out, ln_g, ln_b)


def _hidden_chunks(hidden, n_chunks):
    tiles = hidden // V7X_MXU_DIM
    assert tiles * V7X_MXU_DIM == hidden
    bounds = [V7X_MXU_DIM * ((tiles * c + n_chunks - 1) // n_chunks) for c in range(n_chunks + 1)]
    return tuple(zip(bounds[:-1], bounds[1:]))


def _ffn_kernel(x_ref, mod_ref, win_ref, wout_ref, lng_ref, lnb_ref, *rest, hidden, chunks, sub, alpha,
                kv_rope):
    if kv_rope is None:
        (o_ref,) = rest
    else:
        modn_ref, wkv_ref, kg_ref, cos_ref, sin_ref, o_ref, kt_ref, v_ref = rest
    gate = mod_ref[5:6, :]
    n_sub = x_ref.shape[0] // sub
    h = {}
    y = {}
    normed = []

    def project_kv():
        r, x_new = normed.pop(0)
        _kv_rows(_modulated_bf16(x_new, modn_ref, 0), slice(r * sub, (r + 1) * sub), wkv_ref, kg_ref,
                 cos_ref, sin_ref, kt_ref, v_ref, kv_rope)

    def expand(r, c0, c1):
        if r not in h:
            h[r] = _modulated_bf16(x_ref[r * sub:(r + 1) * sub, :], mod_ref, 3)
        return (jnp.dot(h[r], win_ref[:, c0:c1], preferred_element_type=F32),
                jnp.dot(h[r], win_ref[:, hidden + c0:hidden + c1], preferred_element_type=F32))

    def contract(gt_up, r, c0, c1):
        gt, up = gt_up
        a = (_silu(gt) * up).astype(BF16)
        part = jnp.dot(a, wout_ref[c0:c1, :], preferred_element_type=F32)
        y[r] = part if r not in y else y[r] + part
        if c1 == hidden:
            rows = slice(r * sub, (r + 1) * sub)
            x_new = _layernorm(alpha * x_ref[rows, :] + gate * y.pop(r), lng_ref[1:2, :], lnb_ref[1:2, :])
            o_ref[rows, :] = x_new
            if kv_rope is not None:
                normed.append((r, x_new))

    units = [(r, c0, c1) for r in range(n_sub) for c0, c1 in chunks]
    pending = expand(*units[0])
    for u, unit in enumerate(units):
        current = pending
        if u + 1 < len(units):
            pending = expand(*units[u + 1])
        if normed:
            project_kv()
        contract(current, *unit)
    if normed:
        project_kv()


def _ffn(x, mod, mod_row, layer, w_in, w_out, ln_g, ln_b, tile, alpha, kv=None):
    bsz, n, d = x.shape
    hidden = w_out.shape[1]

    def mod_map(l):
        return (lambda b, i: (l, b, 0, 0)) if mod_row is None else (lambda b, i: (l, mod_row, 0, 0))

    in_specs = [
        pl.BlockSpec((None, tile, d), lambda b, i: (b, i, 0)),
        pl.BlockSpec((None, None, N_MOD, d), mod_map(layer)),
        _resident((None, d, 2 * hidden), lambda b, i: (layer, 0, 0)),
        _resident((None, hidden, d), lambda b, i: (layer, 0, 0)),
        _resident((None, 2, d), lambda b, i: (layer, 0, 0)),
        _resident((None, 2, d), lambda b, i: (layer, 0, 0)),
    ]
    args = [x, mod, w_in, w_out, ln_g, ln_b]
    out_specs = [pl.BlockSpec((None, tile, d), lambda b, i: (b, i, 0))]
    out_shape = [jax.ShapeDtypeStruct((bsz, n, d), F32)]
    kv_rope = None
    if kv is not None:
        w_qkv, k_g, cos, sin_signed, o, kv_rope = kv
        n_q = N_HEADS * HEAD_DIM
        n_kv = N_KV_HEADS * HEAD_DIM
        assert n_q % (2 * n_kv) == 0
        in_specs += [
            pl.BlockSpec((None, None, N_MOD, d), mod_map(layer + 1)),
            _resident((None, d, 2 * n_kv), lambda b, i: (o, 0, n_q // (2 * n_kv))),
            _resident((None, 1, HEAD_DIM), lambda b, i: (o, 0, 0)),
            pl.BlockSpec((tile, HEAD_DIM), (lambda b, i: (i, 0)) if kv_rope else (lambda b, i: (0, 0))),
            pl.BlockSpec((tile, HEAD_DIM), (lambda b, i: (i, 0)) if kv_rope else (lambda b, i: (0, 0))),
        ]
        args += [mod, w_qkv, k_g, cos, sin_signed]
        out_specs += [pl.BlockSpec((None, n_kv, tile), lambda b, i: (b, 0, i)),
                      pl.BlockSpec((None, tile, 2 * n_kv), lambda b, i: (b, i, 0))]
        out_shape += [jax.ShapeDtypeStruct((bsz, n_kv, n), BF16),
                      jax.ShapeDtypeStruct((bsz, n, 2 * n_kv), BF16)]
    return pl.pallas_call(
        functools.partial(_ffn_kernel, hidden=hidden, chunks=_hidden_chunks(hidden, FFN_HIDDEN_CHUNKS),
                          sub=min(FFN_SUB_ROWS, tile), alpha=alpha, kv_rope=kv_rope),
        grid=(bsz, n // tile),
        in_specs=in_specs,
        out_specs=out_specs,
        out_shape=out_shape,
        compiler_params=_params(2),
        name="ffn" if kv is None else "ffn_kv",
    )(*args)


def _rms_head(xh, g):
    return xh * lax.rsqrt(jnp.mean(xh * xh, axis=-1, keepdims=True) + RMS_EPS) * g


def _rope_head(xh, cos, sin_signed, swap_lo):
    partner = jnp.where(swap_lo, pltpu.roll(xh, HEAD_DIM - HEAD_DIM // 4, axis=1),
                        pltpu.roll(xh, HEAD_DIM // 4, axis=1))
    return xh * cos + partner * sin_signed


def _swap_lo_mask(rows):
    lane = lax.broadcasted_iota(jnp.int32, (rows, HEAD_DIM), 1)
    return (lane % (HEAD_DIM // 2)) < (HEAD_DIM // 4)


def _kv_rows(h, rows, w_ref, kg_ref, cos_ref, sin_ref, kt_ref, v_ref, with_rope):
    n_kv = N_KV_HEADS * HEAD_DIM
    n_rows = h.shape[0]
    pkv = jnp.dot(h, w_ref[...], preferred_element_type=F32)
    for hh in range(N_KV_HEADS):
        sl = slice(hh * HEAD_DIM, (hh + 1) * HEAD_DIM)
        kh = _rms_head(pkv[:, sl], kg_ref[...])
        if with_rope:
            kh = _rope_head(kh, cos_ref[rows, :], sin_ref[rows, :], _swap_lo_mask(n_rows))
        kt_ref[sl, rows] = kh.T.astype(BF16)
        vh = pkv[:, n_kv + hh * HEAD_DIM:n_kv + (hh + 1) * HEAD_DIM]
        v_ref[rows, 2 * hh * HEAD_DIM:(2 * hh + 1) * HEAD_DIM] = vh.astype(BF16)
        v_ref[rows, (2 * hh + 1) * HEAD_DIM:(2 * hh + 2) * HEAD_DIM] = jnp.ones((n_rows, HEAD_DIM), BF16)


def _rope_tables(n):
    rows = n // GRID_W
    row = jnp.repeat(jnp.arange(rows, dtype=F32), GRID_W)
    col = jnp.tile(jnp.arange(GRID_W, dtype=F32), rows)
    n_freq = HEAD_DIM // 4
    inv = ROPE_THETA ** (-jnp.arange(n_freq, dtype=F32) / n_freq)
    ang_r = row[:, None] * inv
    ang_c = col[:, None] * inv
    cos = jnp.concatenate([jnp.cos(ang_r), jnp.cos(ang_r), jnp.cos(ang_c), jnp.cos(ang_c)], axis=-1)
    sin = jnp.concatenate([-jnp.sin(ang_r), jnp.sin(ang_r), -jnp.sin(ang_c), jnp.sin(ang_c)], axis=-1)
    return cos, sin


def _attn_kernel(x_ref, mod_ref, wq_ref, qg_ref, cos_ref, sin_ref, ktl_ref, vl_ref, ktc_ref, vc_ref,
                 wout_ref, lng_ref, lnb_ref, o_ref, o_scr, *, tile, alpha):
    n_sub = o_scr.shape[0]
    sub = tile // n_sub
    swap_lo = _swap_lo_mask(sub)
    q_scale = HEAD_DIM ** -0.5 * math.log2(math.e)
    gate = mod_ref[2:3, :]

    def q_proj(r):
        h = _modulated_bf16(x_ref[r * sub:(r + 1) * sub, :], mod_ref, 0)
        return jnp.dot(h, wq_ref[...], preferred_element_type=F32)

    def scores(pq, r, hh):
        rows = slice(r * sub, (r + 1) * sub)
        ksl = slice(hh // Q_PER_KV * HEAD_DIM, (hh // Q_PER_KV + 1) * HEAD_DIM)
        qh = _rope_head(_rms_head(pq[:, hh * HEAD_DIM:(hh + 1) * HEAD_DIM], qg_ref[...]),
                        cos_ref[rows, :], sin_ref[rows, :], swap_lo)
        qh = (qh * q_scale).astype(BF16)
        return (jnp.dot(qh, ktl_ref[ksl, :], preferred_element_type=F32),
                jnp.dot(qh, ktc_ref[ksl, :], preferred_element_type=F32))

    def attend(s, r, hh):
        s_lat, s_ctx = s
        vsl = slice(2 * (hh // Q_PER_KV) * HEAD_DIM, (2 * (hh // Q_PER_KV) + 2) * HEAD_DIM)
        m = jnp.maximum(jnp.max(s_lat, axis=-1, keepdims=True),
                        jnp.max(s_ctx, axis=-1, keepdims=True))
        p_lat = jnp.exp2(s_lat - m).astype(BF16)
        p_ctx = jnp.exp2(s_ctx - m).astype(BF16)
        o = (jnp.dot(p_lat, vl_ref[:, vsl], preferred_element_type=F32)
             + jnp.dot(p_ctx, vc_ref[:, vsl], preferred_element_type=F32))
        o_scr[r, :, hh * HEAD_DIM:(hh + 1) * HEAD_DIM] = (
            o[:, 0:HEAD_DIM] / o[:, HEAD_DIM:HEAD_DIM + 1]).astype(BF16)

    def out_proj(r):
        rows = slice(r * sub, (r + 1) * sub)
        y = jnp.dot(o_scr[r], wout_ref[...], preferred_element_type=F32)
        o_ref[rows, :] = _layernorm(alpha * x_ref[rows, :] + gate * y, lng_ref[0:1, :], lnb_ref[0:1, :])

    units = [(r, hh) for r in range(n_sub) for hh in range(N_HEADS)]
    pq = {0: q_proj(0)}
    ahead = [scores(pq[r], r, hh) for r, hh in units[:ATTN_SCORES_AHEAD]]
    for u, (r, hh) in enumerate(units):
        if hh == N_HEADS // 2 and r + 1 < n_sub:
            pq[r + 1] = q_proj(r + 1)
        if u + ATTN_SCORES_AHEAD < len(units):
            r_next, hh_next = units[u + ATTN_SCORES_AHEAD]
            ahead.append(scores(pq[r_next], r_next, hh_next))
        attend(ahead.pop(0), r, hh)
        if hh == 1 and r > 0:
            out_proj(r - 1)
    out_proj(n_sub - 1)


def _attention(x, mod, layer, w_qkv, q_g, cos, sin_signed, kt_lat, v_lat, kt_ctx, v_ctx, n_ctx, w_out,
               ln_g, ln_b, o, tile, alpha):
    bsz, n, d = x.shape
    n_q = N_HEADS * HEAD_DIM
    n_kv = N_KV_HEADS * HEAD_DIM
    return pl.pallas_call(
        functools.partial(_attn_kernel, tile=tile, alpha=alpha),
        grid=(bsz, n // tile),
        in_specs=[
            pl.BlockSpec((None, tile, d), lambda b, i: (b, i, 0)),
            pl.BlockSpec((None, None, N_MOD, d), lambda b, i: (layer, b, 0, 0)),
            _resident((None, d, n_q), lambda b, i: (o, 0, 0)),
            _resident((None, 1, HEAD_DIM), lambda b, i: (o, 0, 0)),
            pl.BlockSpec((tile, HEAD_DIM), lambda b, i: (i, 0)),
            pl.BlockSpec((tile, HEAD_DIM), lambda b, i: (i, 0)),
            pl.BlockSpec((None, n_kv, n), lambda b, i: (b, 0, 0)),
            pl.BlockSpec((None, n, 2 * n_kv), lambda b, i: (b, 0, 0)),
            pl.BlockSpec((None, n_kv, n_ctx), lambda b, i: (0, 0, b)),
            pl.BlockSpec((None, n_ctx, 2 * n_kv), lambda b, i: (0, b, 0)),
            _resident((None, n_q, d), lambda b, i: (o, 0, 0)),
            _resident((None, 2, d), lambda b, i: (layer, 0, 0)),
            _resident((None, 2, d), lambda b, i: (layer, 0, 0)),
        ],
        out_specs=pl.BlockSpec((None, tile, d), lambda b, i: (b, i, 0)),
        out_shape=jax.ShapeDtypeStruct((bsz, n, d), F32),
        scratch_shapes=[pltpu.VMEM((tile // ATTN_SUB_ROWS, ATTN_SUB_ROWS, n_q), BF16)],
        compiler_params=_params(2),
        name="attention",
    )(x, mod, w_qkv, q_g, cos, sin_signed, kt_lat, v_lat, kt_ctx, v_ctx, w_out, ln_g, ln_b)


def kernel(x, c, ctx, c_ctx, ada_w, ada_b, ln_g, ln_b, ffn_w_in, ffn_w_out, mix_w_in, conv_w,
           sgu_ln_g, sgu_ln_b, sgu_w, sgu_b, mix_w_out, attn_w_qkv, q_norm_g, k_norm_g, attn_w_out):
    bsz, seq, d = x.shape
    n_ctx = ctx.shape[1]
    depth = ada_w.shape[0]
    alpha = (2.0 * depth) ** 0.25
    assert seq % TILE_MIX_IN == 0 and seq % TILE_MIX_OUT == 0 and seq % TILE_ATTN == 0 and seq % TILE_FFN == 0
    assert n_ctx % CHUNK == 0 and seq % GRID_W == 0 and (bsz * n_ctx) % TILE_FFN == 0
    tile_ctx = n_ctx

    pad = (-(bsz + 1)) % MOD_ROWS_PAD
    c_all = jnp.concatenate([c, c_ctx[None, :], jnp.zeros((pad, d), F32)], axis=0)
    ctx_row = bsz
    mod = _ada_modulation(c_all, ada_w, ada_b)

    ffn_w_in_b = ffn_w_in.astype(BF16)
    ffn_w_out_b = ffn_w_out.astype(BF16)
    mix_w_in_b = mix_w_in.astype(BF16)
    mix_w_out_b = mix_w_out.astype(BF16)
    attn_w_qkv_b = attn_w_qkv.astype(BF16)
    attn_w_out_b = attn_w_out.astype(BF16)
    n_even, n_groups = sgu_w.shape[0], sgu_w.shape[1]
    cw = sgu_ln_g.shape[-1]
    sgu_wcat = jnp.transpose(sgu_w, (0, 2, 1, 3)).reshape(n_even, CHUNK, n_groups * CHUNK).astype(BF16)
    sgu_bias = jnp.repeat(jnp.transpose(sgu_b, (0, 2, 1)), cw // n_groups, axis=-1)
    sgu_ln_g3 = sgu_ln_g.reshape(n_even, 1, cw)
    sgu_ln_b3 = sgu_ln_b.reshape(n_even, 1, cw)
    q_g3 = q_norm_g.reshape(-1, 1, HEAD_DIM)
    k_g3 = k_norm_g.reshape(-1, 1, HEAD_DIM)
    cos, sin_signed = _rope_tables(seq)

    for l in range(depth):
        last = l == depth - 1
        if l % 2 == 0:
            e = l // 2
            mix_args = (conv_w, sgu_wcat, sgu_bias, mix_w_out_b, ln_g, ln_b, e)
            q_lat = _mix_in(x, mod, None, l, mix_w_in_b, sgu_ln_g3, sgu_ln_b3, e, TILE_MIX_IN)
            x_mid = _mix_out(q_lat, x, mod, None, l, *mix_args, TILE_MIX_OUT, alpha)
            if not last:
                q_ctx = _mix_in(ctx, mod, ctx_row, l, mix_w_in_b, sgu_ln_g3, sgu_ln_b3, e, tile_ctx)
                ctx_mid = _mix_out(q_ctx, ctx, mod, ctx_row, l, *mix_args, tile_ctx, alpha)
        else:
            o = l // 2
            assert last, "an attention layer that also updates the context stream is not implemented"
            x_mid = _attention(x, mod, l, attn_w_qkv_b, q_g3, cos, sin_signed, *kv_lat, *kv_ctx, n_ctx,
                               attn_w_out_b, ln_g, ln_b, o, TILE_ATTN, alpha)
        ffn_args = (ffn_w_in_b, ffn_w_out_b, ln_g, ln_b, TILE_FFN, alpha)
        if last:
            (x,) = _ffn(x_mid, mod, None, l, *ffn_args)
        else:
            o_next = (l + 1) // 2
            kv_args = (attn_w_qkv_b, k_g3, cos, sin_signed, o_next)
            x, *kv_lat = _ffn(x_mid, mod, None, l, *ffn_args, kv=kv_args + (True,))
            ctx_flat, *kv_ctx = _ffn(ctx_mid.reshape(1, bsz * n_ctx, d), mod, ctx_row, l, *ffn_args,
                                     kv=kv_args + (False,))
            ctx = ctx_flat.reshape(bsz, n_ctx, d)
    return x
```

```python
import functools
import math

import jax
import jax.numpy as jnp
from jax import lax
from jax.experimental import pallas as pl
from jax.experimental.pallas import tpu as pltpu

F32 = jnp.float32
BF16 = jnp.bfloat16

GRID_W = 64
N_HEADS = 8
N_KV_HEADS = 2
Q_PER_KV = N_HEADS // N_KV_HEADS
HEAD_DIM = 128
ROPE_THETA = 10000.0
CONV_TAPS = 3
SGU_GROUPS = 8
CHUNK = 128
LN_EPS = 1e-5
RMS_EPS = 1e-6
N_MOD = 6
MOD_ROWS_PAD = 8

V7X_VMEM_LIMIT_BYTES = 56 * 1024 * 1024
V7X_MXU_DIM = 256
BF16_SUBLANES = 16

TILE_MIX_IN = 1024
TILE_MIX_OUT = 512
TILE_FFN = 1024
TILE_ATTN = 1024
FFN_SUB_ROWS = 256
MIX_SUB_ROWS = 256
ATTN_SUB_ROWS = 512
ATTN_SCORES_AHEAD = 1
FFN_HIDDEN_CHUNKS = 3


def _params(n_grid_dims):
    return pltpu.CompilerParams(
        dimension_semantics=("arbitrary",) * n_grid_dims,
        vmem_limit_bytes=V7X_VMEM_LIMIT_BYTES)


def _resident(shape, index_map):
    return pl.BlockSpec(shape, index_map, pipeline_mode=pl.Buffered(1))


def _layernorm(xf, g, b, eps=LN_EPS):
    mu = jnp.mean(xf, axis=-1, keepdims=True)
    xc = xf - mu
    var = jnp.mean(xc * xc, axis=-1, keepdims=True)
    return xc * lax.rsqrt(var + eps) * g + b


def _post_norm(x, y, gate, g, b, alpha):
    return _layernorm(x + (gate * (1.0 / alpha)) * y, g, b, LN_EPS / (alpha * alpha))


def _gelu(x):
    return 0.5 * x * (1.0 + lax.erf(x * (1.0 / math.sqrt(2.0))))


def _silu(x):
    return x * jax.nn.sigmoid(x)


def _modulated_bf16(x, mod_ref, shift_row):
    shift = mod_ref[shift_row:shift_row + 1, :]
    scale = mod_ref[shift_row + 1:shift_row + 2, :]
    return (x * (1.0 + scale) + shift).astype(BF16)


def _ada_kernel(c_ref, w_ref, b_ref, o_ref):
    s = _silu(c_ref[...])
    o_ref[...] = jnp.dot(s, w_ref[...], preferred_element_type=F32,
                         precision=lax.Precision.HIGHEST) + b_ref[...]


def _ada_modulation(c_all, ada_w, ada_b):
    depth, d, n6 = ada_w.shape
    rows = c_all.shape[0]
    tn = d
    out = pl.pallas_call(
        _ada_kernel,
        grid=(depth, n6 // tn),
        in_specs=[
            pl.BlockSpec((rows, d), lambda l, j: (0, 0)),
            pl.BlockSpec((None, d, tn), lambda l, j: (l, 0, j)),
            pl.BlockSpec((None, 1, tn), lambda l, j: (l, 0, j)),
        ],
        out_specs=pl.BlockSpec((None, rows, tn), lambda l, j: (l, 0, j)),
        out_shape=jax.ShapeDtypeStruct((depth, rows, n6), F32),
        compiler_params=_params(2),
        name="ada_modulation",
    )(c_all, ada_w, ada_b.reshape(depth, 1, n6))
    return out.reshape(depth, rows, N_MOD, d)


def _mix_in_kernel(x_ref, mod_ref, w_ref, lng_ref, lnb_ref, o_ref, *, cw, sub):
    for r in range(x_ref.shape[0] // sub):
        rows = slice(r * sub, (r + 1) * sub)
        h = _modulated_bf16(x_ref[rows, :], mod_ref, 0)
        p = jnp.dot(h, w_ref[...], preferred_element_type=F32)
        o_ref[rows, 0:cw] = p[:, 0:cw].astype(BF16)
        o_ref[rows, cw:2 * cw] = (p[:, cw:2 * cw] * p[:, 2 * cw:3 * cw]).astype(BF16)
        o_ref[rows, 2 * cw:3 * cw] = _gelu(p[:, 3 * cw:4 * cw]).astype(BF16)
        v = _layernorm(_gelu(p[:, 4 * cw:5 * cw]), lng_ref[...], lnb_ref[...])
        o_ref[rows, 3 * cw:4 * cw] = v.astype(BF16)


def _mix_in(x, mod, mod_row, layer, w_in, sgu_ln_g, sgu_ln_b, e, tile):
    bsz, n, d = x.shape
    cw = w_in.shape[-1] // 5
    mod_map = (lambda b, i: (layer, b, 0, 0)) if mod_row is None else (lambda b, i: (layer, mod_row, 0, 0))
    return pl.pallas_call(
        functools.partial(_mix_in_kernel, cw=cw, sub=min(MIX_SUB_ROWS, tile)),
        grid=(bsz, n // tile),
        in_specs=[
            pl.BlockSpec((None, tile, d), lambda b, i: (b, i, 0)),
            pl.BlockSpec((None, None, N_MOD, d), mod_map),
            _resident((None, d, 5 * cw), lambda b, i: (e, 0, 0)),
            _resident((None, 1, cw), lambda b, i: (e, 0, 0)),
            _resident((None, 1, cw), lambda b, i: (e, 0, 0)),
        ],
        out_specs=pl.BlockSpec((None, tile, 4 * cw), lambda b, i: (b, i, 0)),
        out_shape=jax.ShapeDtypeStruct((bsz, n, 4 * cw), BF16),
        compiler_params=_params(2),
        name="mix_in",
    )(x, mod, w_in, sgu_ln_g, sgu_ln_b)


def _mix_out_kernel(q_ref, zprev_ref, znext_ref, x_ref, mod_ref, convw_ref, wcat_ref, diag_ref, sb_ref,
                    wout_ref, lng_ref, lnb_ref, o_ref, *, cw, tile, alpha):
    i = pl.program_id(1)
    n_tiles = pl.num_programs(1)

    z = q_ref[:, cw:2 * cw].astype(F32)
    rows = lax.broadcasted_iota(jnp.int32, (tile, cw), 0)
    prev_row = jnp.where(i > 0, zprev_ref[BF16_SUBLANES - 1:BF16_SUBLANES, :].astype(F32), 0.0)
    next_row = jnp.where(i < n_tiles - 1, znext_ref[0:1, :].astype(F32), 0.0)
    z_before = jnp.where(rows == 0, prev_row, pltpu.roll(z, 1, axis=0))
    z_after = jnp.where(rows == tile - 1, next_row, pltpu.roll(z, tile - 1, axis=0))
    zc = convw_ref[0:1, :] * z_before + convw_ref[1:2, :] * z + convw_ref[2:3, :] * z_after
    y_a = (q_ref[:, 0:cw].astype(F32) * zc).astype(BF16)

    gpt = diag_ref.shape[0] // CHUNK
    y_b_chunks = []
    for c in range(tile // CHUNK):
        sl = slice(c * CHUNK, (c + 1) * CHUNK)
        lane_tiles = []
        for j in range(cw // V7X_MXU_DIM):
            lanes = slice(3 * cw + j * V7X_MXU_DIM, 3 * cw + (j + 1) * V7X_MXU_DIM)
            v = q_ref[sl, lanes]
            vd = jnp.concatenate([v] * gpt, axis=0) * diag_ref[...]
            w = wcat_ref[:, j * gpt * CHUNK:(j + 1) * gpt * CHUNK]
            out = slice(j * V7X_MXU_DIM, (j + 1) * V7X_MXU_DIM)
            s = jnp.dot(w, vd, preferred_element_type=F32) + sb_ref[:, out]
            u = q_ref[sl, 2 * cw + j * V7X_MXU_DIM:2 * cw + (j + 1) * V7X_MXU_DIM]
            lane_tiles.append((u.astype(F32) * s).astype(BF16))
        y_b_chunks.append(jnp.concatenate(lane_tiles, axis=1))
    y_b = jnp.concatenate(y_b_chunks, axis=0)

    y = (jnp.dot(y_a, wout_ref[0:cw, :], preferred_element_type=F32)
         + jnp.dot(y_b, wout_ref[cw:2 * cw, :], preferred_element_type=F32))
    gate = mod_ref[2:3, :]
    o_ref[...] = _post_norm(x_ref[...], y, gate, lng_ref[0:1, :], lnb_ref[0:1, :], alpha)


def _mix_out(q, x, mod, mod_row, layer, conv_w, wcat, sbias, w_out, ln_g, ln_b, e, tile, alpha):
    bsz, n, d = x.shape
    cw = q.shape[-1] // 4
    n_groups = wcat.shape[-1] // CHUNK
    gpt = V7X_MXU_DIM // (cw // n_groups)
    diag = jnp.kron(jnp.eye(gpt, dtype=BF16), jnp.ones((CHUNK, V7X_MXU_DIM // gpt), BF16))
    halo = BF16_SUBLANES
    per_tile = tile // halo
    n_halo_blocks = n // halo
    mod_map = (lambda b, i: (layer, b, 0, 0)) if mod_row is None else (lambda b, i: (layer, mod_row, 0, 0))
    return pl.pallas_call(
        functools.partial(_mix_out_kernel, cw=cw, tile=tile, alpha=alpha),
        grid=(bsz, n // tile),
        in_specs=[
            pl.BlockSpec((None, tile, 4 * cw), lambda b, i: (b, i, 0)),
            pl.BlockSpec((None, halo, cw), lambda b, i: (b, jnp.maximum(i * per_tile - 1, 0), 1)),
            pl.BlockSpec((None, halo, cw),
                         lambda b, i: (b, jnp.minimum((i + 1) * per_tile, n_halo_blocks - 1), 1)),
            pl.BlockSpec((None, tile, d), lambda b, i: (b, i, 0)),
            pl.BlockSpec((None, None, N_MOD, d), mod_map),
            _resident((None, CONV_TAPS, cw), lambda b, i: (e, 0, 0)),
            _resident((None, CHUNK, wcat.shape[-1]), lambda b, i: (e, 0, 0)),
            _resident(diag.shape, lambda b, i: (0, 0)),
            _resident((None, CHUNK, cw), lambda b, i: (e, 0, 0)),
            _resident((None, 2 * cw, d), lambda b, i: (e, 0, 0)),
            _resident((None, 2, d), lambda b, i: (layer, 0, 0)),
            _resident((None, 2, d), lambda b, i: (layer, 0, 0)),
        ],
        out_specs=pl.BlockSpec((None, tile, d), lambda b, i: (b, i, 0)),
        out_shape=jax.ShapeDtypeStruct((bsz, n, d), F32),
        compiler_params=_params(2),
        name="mix_out",
    )(q, q, q, x, mod, conv_w, wcat, diag, sbias, w_out, ln_g, ln_b)


def _hidden_chunks(hidden, n_chunks):
    tiles = hidden // V7X_MXU_DIM
    assert tiles * V7X_MXU_DIM == hidden
    bounds = [V7X_MXU_DIM * ((tiles * c + n_chunks - 1) // n_chunks) for c in range(n_chunks + 1)]
    return tuple(zip(bounds[:-1], bounds[1:]))


def _ffn_kernel(x_ref, mod_ref, win_ref, wout_ref, lng_ref, lnb_ref, *rest, hidden, chunks, sub, alpha,
                kv_rope):
    if kv_rope is None:
        (o_ref,) = rest
    else:
        modn_ref, wkv_ref, kg_ref, cos_ref, sin_ref, o_ref, kt_ref, v_ref = rest
    gate = mod_ref[5:6, :]
    n_sub = x_ref.shape[0] // sub
    h = {}
    y = {}
    normed = []

    def project_kv():
        r, x_new = normed.pop(0)
        _kv_rows(_modulated_bf16(x_new, modn_ref, 0), slice(r * sub, (r + 1) * sub), wkv_ref, kg_ref,
                 cos_ref, sin_ref, kt_ref, v_ref, kv_rope)

    def expand(r, c0, c1):
        if r not in h:
            h[r] = _modulated_bf16(x_ref[r * sub:(r + 1) * sub, :], mod_ref, 3)
        return (jnp.dot(h[r], win_ref[:, c0:c1], preferred_element_type=F32),
                jnp.dot(h[r], win_ref[:, hidden + c0:hidden + c1], preferred_element_type=F32))

    def contract(gt_up, r, c0, c1):
        gt, up = gt_up
        a = (_silu(gt) * up).astype(BF16)
        part = jnp.dot(a, wout_ref[c0:c1, :], preferred_element_type=F32)
        y[r] = part if r not in y else y[r] + part
        if c1 == hidden:
            rows = slice(r * sub, (r + 1) * sub)
            x_new = _post_norm(x_ref[rows, :], y.pop(r), gate, lng_ref[1:2, :], lnb_ref[1:2, :], alpha)
            o_ref[rows, :] = x_new
            if kv_rope is not None:
                normed.append((r, x_new))

    units = [(r, c0, c1) for r in range(n_sub) for c0, c1 in chunks]
    pending = expand(*units[0])
    for u, unit in enumerate(units):
        current = pending
        if u + 1 < len(units):
            pending = expand(*units[u + 1])
        if normed:
            project_kv()
        contract(current, *unit)
    if normed:
        project_kv()


def _ffn(x, mod, mod_row, layer, w_in, w_out, ln_g, ln_b, tile, alpha, kv=None):
    bsz, n, d = x.shape
    hidden = w_out.shape[1]

    def mod_map(l):
        return (lambda b, i: (l, b, 0, 0)) if mod_row is None else (lambda b, i: (l, mod_row, 0, 0))

    in_specs = [
        pl.BlockSpec((None, tile, d), lambda b, i: (b, i, 0)),
        pl.BlockSpec((None, None, N_MOD, d), mod_map(layer)),
        _resident((None, d, 2 * hidden), lambda b, i: (layer, 0, 0)),
        _resident((None, hidden, d), lambda b, i: (layer, 0, 0)),
        _resident((None, 2, d), lambda b, i: (layer, 0, 0)),
        _resident((None, 2, d), lambda b, i: (layer, 0, 0)),
    ]
    args = [x, mod, w_in, w_out, ln_g, ln_b]
    out_specs = [pl.BlockSpec((None, tile, d), lambda b, i: (b, i, 0))]
    out_shape = [jax.ShapeDtypeStruct((bsz, n, d), F32)]
    kv_rope = None
    if kv is not None:
        w_qkv, k_g, cos, sin_signed, o, kv_rope = kv
        n_q = N_HEADS * HEAD_DIM
        n_kv = N_KV_HEADS * HEAD_DIM
        assert n_q % (2 * n_kv) == 0
        in_specs += [
            pl.BlockSpec((None, None, N_MOD, d), mod_map(layer + 1)),
            _resident((None, d, 2 * n_kv), lambda b, i: (o, 0, n_q // (2 * n_kv))),
            _resident((None, 1, HEAD_DIM), lambda b, i: (o, 0, 0)),
            pl.BlockSpec((tile, HEAD_DIM), (lambda b, i: (i, 0)) if kv_rope else (lambda b, i: (0, 0))),
            pl.BlockSpec((tile, HEAD_DIM), (lambda b, i: (i, 0)) if kv_rope else (lambda b, i: (0, 0))),
        ]
        args += [mod, w_qkv, k_g, cos, sin_signed]
        out_specs += [pl.BlockSpec((None, n_kv, tile), lambda b, i: (b, 0, i)),
                      pl.BlockSpec((None, tile, 2 * n_kv), lambda b, i: (b, i, 0))]
        out_shape += [jax.ShapeDtypeStruct((bsz, n_kv, n), BF16),
                      jax.ShapeDtypeStruct((bsz, n, 2 * n_kv), BF16)]
    return pl.pallas_call(
        functools.partial(_ffn_kernel, hidden=hidden, chunks=_hidden_chunks(hidden, FFN_HIDDEN_CHUNKS),
                          sub=min(FFN_SUB_ROWS, tile), alpha=alpha, kv_rope=kv_rope),
        grid=(bsz, n // tile),
        in_specs=in_specs,
        out_specs=out_specs,
        out_shape=out_shape,
        compiler_params=_params(2),
        name="ffn" if kv is None else "ffn_kv",
    )(*args)


def _rms_head(xh, g):
    return xh * lax.rsqrt(jnp.mean(xh * xh, axis=-1, keepdims=True) + RMS_EPS) * g


def _rope_head(xh, cos, sin_signed, swap_lo):
    partner = jnp.where(swap_lo, pltpu.roll(xh, HEAD_DIM - HEAD_DIM // 4, axis=1),
                        pltpu.roll(xh, HEAD_DIM // 4, axis=1))
    return xh * cos + partner * sin_signed


def _swap_lo_mask(rows):
    lane = lax.broadcasted_iota(jnp.int32, (rows, HEAD_DIM), 1)
    return (lane % (HEAD_DIM // 2)) < (HEAD_DIM // 4)


def _kv_rows(h, rows, w_ref, kg_ref, cos_ref, sin_ref, kt_ref, v_ref, with_rope):
    n_kv = N_KV_HEADS * HEAD_DIM
    n_rows = h.shape[0]
    pkv = jnp.dot(h, w_ref[...], preferred_element_type=F32)
    for hh in range(N_KV_HEADS):
        sl = slice(hh * HEAD_DIM, (hh + 1) * HEAD_DIM)
        kh = _rms_head(pkv[:, sl], kg_ref[...])
        if with_rope:
            kh = _rope_head(kh, cos_ref[rows, :], sin_ref[rows, :], _swap_lo_mask(n_rows))
        kt_ref[sl, rows] = kh.T.astype(BF16)
        vh = pkv[:, n_kv + hh * HEAD_DIM:n_kv + (hh + 1) * HEAD_DIM]
        v_ref[rows, 2 * hh * HEAD_DIM:(2 * hh + 1) * HEAD_DIM] = vh.astype(BF16)
        v_ref[rows, (2 * hh + 1) * HEAD_DIM:(2 * hh + 2) * HEAD_DIM] = jnp.ones((n_rows, HEAD_DIM), BF16)


def _rope_tables(n):
    rows = n // GRID_W
    row = jnp.repeat(jnp.arange(rows, dtype=F32), GRID_W)
    col = jnp.tile(jnp.arange(GRID_W, dtype=F32), rows)
    n_freq = HEAD_DIM // 4
    inv = ROPE_THETA ** (-jnp.arange(n_freq, dtype=F32) / n_freq)
    ang_r = row[:, None] * inv
    ang_c = col[:, None] * inv
    cos = jnp.concatenate([jnp.cos(ang_r), jnp.cos(ang_r), jnp.cos(ang_c), jnp.cos(ang_c)], axis=-1)
    sin = jnp.concatenate([-jnp.sin(ang_r), jnp.sin(ang_r), -jnp.sin(ang_c), jnp.sin(ang_c)], axis=-1)
    return cos, sin


def _attn_kernel(x_ref, mod_ref, wq_ref, qg_ref, cos_ref, sin_ref, ktl_ref, vl_ref, ktc_ref, vc_ref,
                 wout_ref, lng_ref, lnb_ref, o_ref, o_scr, *, tile, alpha):
    n_sub = o_scr.shape[0]
    sub = tile // n_sub
    swap_lo = _swap_lo_mask(sub)
    q_scale = HEAD_DIM ** -0.5 * math.log2(math.e)
    gate = mod_ref[2:3, :]

    def q_proj(r):
        h = _modulated_bf16(x_ref[r * sub:(r + 1) * sub, :], mod_ref, 0)
        return jnp.dot(h, wq_ref[...], preferred_element_type=F32)

    def scores(pq, r, hh):
        rows = slice(r * sub, (r + 1) * sub)
        ksl = slice(hh // Q_PER_KV * HEAD_DIM, (hh // Q_PER_KV + 1) * HEAD_DIM)
        qh = _rope_head(_rms_head(pq[:, hh * HEAD_DIM:(hh + 1) * HEAD_DIM], qg_ref[...]),
                        cos_ref[rows, :], sin_ref[rows, :], swap_lo)
        qh = (qh * q_scale).astype(BF16)
        return (jnp.dot(qh, ktl_ref[ksl, :], preferred_element_type=F32),
                jnp.dot(qh, ktc_ref[ksl, :], preferred_element_type=F32))

    def attend(s, r, hh):
        s_lat, s_ctx = s
        vsl = slice(2 * (hh // Q_PER_KV) * HEAD_DIM, (2 * (hh // Q_PER_KV) + 2) * HEAD_DIM)
        m = jnp.maximum(jnp.max(s_lat, axis=-1, keepdims=True),
                        jnp.max(s_ctx, axis=-1, keepdims=True))
        p_lat = jnp.exp2(s_lat - m).astype(BF16)
        p_ctx = jnp.exp2(s_ctx - m).astype(BF16)
        o = (jnp.dot(p_lat, vl_ref[:, vsl], preferred_element_type=F32)
             + jnp.dot(p_ctx, vc_ref[:, vsl], preferred_element_type=F32))
        o_scr[r, :, hh * HEAD_DIM:(hh + 1) * HEAD_DIM] = (
            o[:, 0:HEAD_DIM] / o[:, HEAD_DIM:HEAD_DIM + 1]).astype(BF16)

    def out_proj(r):
        rows = slice(r * sub, (r + 1) * sub)
        y = jnp.dot(o_scr[r], wout_ref[...], preferred_element_type=F32)
        o_ref[rows, :] = _post_norm(x_ref[rows, :], y, gate, lng_ref[0:1, :], lnb_ref[0:1, :], alpha)

    units = [(r, hh) for r in range(n_sub) for hh in range(N_HEADS)]
    pq = {0: q_proj(0)}
    ahead = [scores(pq[r], r, hh) for r, hh in units[:ATTN_SCORES_AHEAD]]
    for u, (r, hh) in enumerate(units):
        if hh == N_HEADS // 2 and r + 1 < n_sub:
            pq[r + 1] = q_proj(r + 1)
        if u + ATTN_SCORES_AHEAD < len(units):
            r_next, hh_next = units[u + ATTN_SCORES_AHEAD]
            ahead.append(scores(pq[r_next], r_next, hh_next))
        attend(ahead.pop(0), r, hh)
        if hh == 1 and r > 0:
            out_proj(r - 1)
    out_proj(n_sub - 1)


def _attention(x, mod, layer, w_qkv, q_g, cos, sin_signed, kt_lat, v_lat, kt_ctx, v_ctx, n_ctx, w_out,
               ln_g, ln_b, o, tile, alpha):
    bsz, n, d = x.shape
    n_q = N_HEADS * HEAD_DIM
    n_kv = N_KV_HEADS * HEAD_DIM
    return pl.pallas_call(
        functools.partial(_attn_kernel, tile=tile, alpha=alpha),
        grid=(bsz, n // tile),
        in_specs=[
            pl.BlockSpec((None, tile, d), lambda b, i: (b, i, 0)),
            pl.BlockSpec((None, None, N_MOD, d), lambda b, i: (layer, b, 0, 0)),
            _resident((None, d, n_q), lambda b, i: (o, 0, 0)),
            _resident((None, 1, HEAD_DIM), lambda b, i: (o, 0, 0)),
            pl.BlockSpec((tile, HEAD_DIM), lambda b, i: (i, 0)),
            pl.BlockSpec((tile, HEAD_DIM), lambda b, i: (i, 0)),
            pl.BlockSpec((None, n_kv, n), lambda b, i: (b, 0, 0)),
            pl.BlockSpec((None, n, 2 * n_kv), lambda b, i: (b, 0, 0)),
            pl.BlockSpec((None, n_kv, n_ctx), lambda b, i: (0, 0, b)),
            pl.BlockSpec((None, n_ctx, 2 * n_kv), lambda b, i: (0, b, 0)),
            _resident((None, n_q, d), lambda b, i: (o, 0, 0)),
            _resident((None, 2, d), lambda b, i: (layer, 0, 0)),
            _resident((None, 2, d), lambda b, i: (layer, 0, 0)),
        ],
        out_specs=pl.BlockSpec((None, tile, d), lambda b, i: (b, i, 0)),
        out_shape=jax.ShapeDtypeStruct((bsz, n, d), F32),
        scratch_shapes=[pltpu.VMEM((tile // ATTN_SUB_ROWS, ATTN_SUB_ROWS, n_q), BF16)],
        compiler_params=_params(2),
        name="attention",
    )(x, mod, w_qkv, q_g, cos, sin_signed, kt_lat, v_lat, kt_ctx, v_ctx, w_out, ln_g, ln_b)


def kernel(x, c, ctx, c_ctx, ada_w, ada_b, ln_g, ln_b, ffn_w_in, ffn_w_out, mix_w_in, conv_w,
           sgu_ln_g, sgu_ln_b, sgu_w, sgu_b, mix_w_out, attn_w_qkv, q_norm_g, k_norm_g, attn_w_out):
    bsz, seq, d = x.shape
    n_ctx = ctx.shape[1]
    depth = ada_w.shape[0]
    alpha = (2.0 * depth) ** 0.25
    assert seq % TILE_MIX_IN == 0 and seq % TILE_MIX_OUT == 0 and seq % TILE_ATTN == 0 and seq % TILE_FFN == 0
    assert n_ctx % CHUNK == 0 and seq % GRID_W == 0 and (bsz * n_ctx) % TILE_FFN == 0
    tile_ctx = n_ctx

    pad = (-(bsz + 1)) % MOD_ROWS_PAD
    c_all = jnp.concatenate([c, c_ctx[None, :], jnp.zeros((pad, d), F32)], axis=0)
    ctx_row = bsz
    mod = _ada_modulation(c_all, ada_w, ada_b)

    ffn_w_in_b = ffn_w_in.astype(BF16)
    ffn_w_out_b = ffn_w_out.astype(BF16)
    mix_w_in_b = mix_w_in.astype(BF16)
    mix_w_out_b = mix_w_out.astype(BF16)
    attn_w_qkv_b = attn_w_qkv.astype(BF16)
    attn_w_out_b = attn_w_out.astype(BF16)
    n_even, n_groups = sgu_w.shape[0], sgu_w.shape[1]
    cw = sgu_ln_g.shape[-1]
    sgu_wcat = jnp.transpose(sgu_w, (0, 2, 1, 3)).reshape(n_even, CHUNK, n_groups * CHUNK).astype(BF16)
    sgu_bias = jnp.repeat(jnp.transpose(sgu_b, (0, 2, 1)), cw // n_groups, axis=-1)
    sgu_ln_g3 = sgu_ln_g.reshape(n_even, 1, cw)
    sgu_ln_b3 = sgu_ln_b.reshape(n_even, 1, cw)
    q_g3 = q_norm_g.reshape(-1, 1, HEAD_DIM)
    k_g3 = k_norm_g.reshape(-1, 1, HEAD_DIM)
    cos, sin_signed = _rope_tables(seq)

    for l in range(depth):
        last = l == depth - 1
        if l % 2 == 0:
            e = l // 2
            mix_args = (conv_w, sgu_wcat, sgu_bias, mix_w_out_b, ln_g, ln_b, e)
            q_lat = _mix_in(x, mod, None, l, mix_w_in_b, sgu_ln_g3, sgu_ln_b3, e, TILE_MIX_IN)
            x_mid = _mix_out(q_lat, x, mod, None, l, *mix_args, TILE_MIX_OUT, alpha)
            if not last:
                q_ctx = _mix_in(ctx.reshape(1, bsz * n_ctx, d), mod, ctx_row, l, mix_w_in_b, sgu_ln_g3,
                                sgu_ln_b3, e, TILE_MIX_IN).reshape(bsz, n_ctx, -1)
                ctx_mid = _mix_out(q_ctx, ctx, mod, ctx_row, l, *mix_args, tile_ctx, alpha)
        else:
            o = l // 2
            assert last, "an attention layer that also updates the context stream is not implemented"
            x_mid = _attention(x, mod, l, attn_w_qkv_b, q_g3, cos, sin_signed, *kv_lat, *kv_ctx, n_ctx,
                               attn_w_out_b, ln_g, ln_b, o, TILE_ATTN, alpha)
        ffn_args = (ffn_w_in_b, ffn_w_out_b, ln_g, ln_b, TILE_FFN, alpha)
        if last:
            (x,) = _ffn(x_mid, mod, None, l, *ffn_args)
        else:
            o_next = (l + 1) // 2
            kv_args = (attn_w_qkv_b, k_g3, cos, sin_signed, o_next)
            x, *kv_lat = _ffn(x_mid, mod, None, l, *ffn_args, kv=kv_args + (True,))
            ctx_flat, *kv_ctx = _ffn(ctx_mid.reshape(1, bsz * n_ctx, d), mod, ctx_row, l, *ffn_args,
                                     kv=kv_args + (False,))
            ctx = ctx_flat.reshape(bsz, n_ctx, d)
    return x
```
